```python
import jax, jax.numpy as jnp
from jax import lax
import numpy as np

D_MODEL = 1024
BATCH = 8
SEQ = 2048
DEPTH = 2

N_EVEN = (DEPTH + 1) // 2
N_ODD = DEPTH // 2

CONV_DIM = D_MODEL // 2
CONV_WIDTH = 3
HGRN_HEADS = 4
HGRN_EXPAND = 128
HGRN_HEAD_V = (D_MODEL // 2) // HGRN_HEADS
HGRN_KEY_DIM = HGRN_HEADS * HGRN_EXPAND
HGRN_VAL_DIM = HGRN_HEADS * HGRN_HEAD_V
HGRN_CHUNK = 64
EVEN_SPLITS = [CONV_DIM] * 3 + [HGRN_KEY_DIM] * 2 + [HGRN_VAL_DIM] * 2
EVEN_IN = sum(EVEN_SPLITS)
EVEN_MIX = CONV_DIM + HGRN_VAL_DIM

RET_HEADS = 4
RET_HEAD_K = D_MODEL // RET_HEADS
RET_HEAD_V = 2 * RET_HEAD_K
RET_K_DIM = RET_HEADS * RET_HEAD_K
RET_V_DIM = RET_HEADS * RET_HEAD_V
RET_CHUNK = 128
ROPE_BASE = 10000.0
ODD_SPLITS = [RET_K_DIM, RET_K_DIM, RET_V_DIM, RET_V_DIM]
ODD_IN = sum(ODD_SPLITS)

FFN_HIDDEN = -(-(8 * D_MODEL) // (3 * 256)) * 256

EPS = 1e-6

kernel_name = "hybrid_conv_hgrn2_retention_trunk"


def _split_points(sizes):
    return [int(v) for v in np.cumsum(sizes)[:-1]]


def rmsnorm(x, w):
    xf = x.astype(jnp.float32)
    y = xf * lax.rsqrt(jnp.mean(xf * xf, axis=-1, keepdims=True) + EPS)
    return (y * w.astype(jnp.float32)).astype(x.dtype)


def short_conv_mixer(a, b, c, w_conv):
    S = a.shape[1]
    u = c * a
    up = jnp.pad(u, ((0, 0), (CONV_WIDTH - 1, 0), (0, 0)))
    y = up[:, 0:S] * w_conv[0]
    for k in range(1, CONV_WIDTH):
        y = y + up[:, k:k + S] * w_conv[k]
    return b * y


def hgrn2_mixer(q, f, i, g, lb, norm_w):
    Bn, S, _ = q.shape
    N = S // HGRN_CHUNK
    C = HGRN_CHUNK
    f32 = jnp.float32
    qf = jax.nn.silu(q.astype(f32))
    logf = jnp.logaddexp(jnp.log(lb), jnp.log1p(-lb) + jax.nn.log_sigmoid(f.astype(f32)))
    kf = -jnp.expm1(logf)
    vf = i.astype(f32)

    def to_chunks(t, hd):
        return t.reshape(Bn, N, C, HGRN_HEADS, hd).transpose(1, 0, 3, 2, 4)

    qc = to_chunks(qf, HGRN_EXPAND)
    kc = to_chunks(kf, HGRN_EXPAND)
    lc = to_chunks(logf, HGRN_EXPAND)
    vc = to_chunks(vf, HGRN_HEAD_V)
    causal = jnp.tril(jnp.ones((C, C), dtype=bool))

    def step(state, inp):
        qn, kn, ln, vn = inp
        b = jnp.cumsum(ln, axis=-2)
        o_inter = jnp.einsum('bhck,bhkv->bhcv', qn * jnp.exp(b), state)
        diff = b[:, :, :, None, :] - b[:, :, None, :, :]
        decay = jnp.exp(jnp.where(causal[:, :, None], diff, -jnp.inf))
        scores = jnp.einsum('bhik,bhijk,bhjk->bhij', qn, decay, kn)
        o_intra = jnp.einsum('bhij,bhjv->bhiv', scores, vn)
        b_last = b[:, :, -1:, :]
        new_state = (jnp.exp(b_last[:, :, 0, :])[..., None] * state
                     + jnp.einsum('bhck,bhcv->bhkv', kn * jnp.exp(b_last - b), vn))
        return new_state, o_inter + o_intra

    state0 = jnp.zeros((Bn, HGRN_HEADS, HGRN_EXPAND, HGRN_HEAD_V), f32)
    _, o = lax.scan(step, state0, (qc, kc, lc, vc))
    o = o.transpose(1, 0, 3, 2, 4).reshape(Bn, S, HGRN_HEADS, HGRN_HEAD_V)
    o = o * lax.rsqrt(jnp.mean(o * o, axis=-1, keepdims=True) + EPS)
    o = o * norm_w.astype(f32).reshape(HGRN_HEADS, HGRN_HEAD_V)
    o = o.reshape(Bn, S, HGRN_VAL_DIM) * jax.nn.silu(g.astype(f32))
    return o.astype(q.dtype)


def rotary(x, pos):
    half = x.shape[-1] // 2
    inv_freq = ROPE_BASE ** (-jnp.arange(half, dtype=jnp.float32) / half)
    ang = pos.astype(jnp.float32)[..., None] * inv_freq
    cos = jnp.cos(ang)[:, :, None, :]
    sin = jnp.sin(ang)[:, :, None, :]
    x1, x2 = x[..., :half], x[..., half:]
    return jnp.concatenate([x1 * cos - x2 * sin, x1 * sin + x2 * cos], axis=-1)


def retention_mixer(q, k, v, g, pos, norm_w):
    Bn, S, _ = q.shape
    C = RET_CHUNK
    N = S // C
    f32 = jnp.float32
    qh = rotary(q.astype(f32).reshape(Bn, S, RET_HEADS, RET_HEAD_K), pos)
    kh = rotary(k.astype(f32).reshape(Bn, S, RET_HEADS, RET_HEAD_K), pos) * (RET_HEAD_K ** -0.5)
    vh = v.astype(f32).reshape(Bn, S, RET_HEADS, RET_HEAD_V)
    log_gamma = jnp.log1p(-jnp.exp2(-5.0 - jnp.arange(RET_HEADS, dtype=f32)))
    idx = jnp.arange(C, dtype=f32)
    rel = idx[:, None] - idx[None, :]
    d_mat = jnp.where(rel >= 0, jnp.exp(log_gamma[:, None, None] * jnp.maximum(rel, 0.0)), 0.0)

    qc = qh.reshape(Bn, N, C, RET_HEADS, RET_HEAD_K)
    kc = kh.reshape(Bn, N, C, RET_HEADS, RET_HEAD_K)
    vc = vh.reshape(Bn, N, C, RET_HEADS, RET_HEAD_V)
    scores = jnp.einsum('bnihd,bnjhd->bnhij', qc, kc) * d_mat
    o_intra = jnp.einsum('bnhij,bnjhv->bnihv', scores, vc)

    q_dec = qc * jnp.exp(log_gamma[None, :] * (idx[:, None] + 1.0))[:, :, None]
    k_dec = kc * jnp.exp(log_gamma[None, :] * (C - 1.0 - idx[:, None]))[:, :, None]
    chunk_decay = jnp.exp(log_gamma * C)[None, :, None, None]

    def step(state, inp):
        qd, kd, vn = inp
        o = jnp.einsum('bchk,bhkv->bchv', qd, state)
        new_state = chunk_decay * state + jnp.einsum('bchk,bchv->bhkv', kd, vn)
        return new_state, o

    state0 = jnp.zeros((Bn, RET_HEADS, RET_HEAD_K, RET_HEAD_V), f32)
    _, o_inter = lax.scan(step, state0, (q_dec.transpose(1, 0, 2, 3, 4),
                                         k_dec.transpose(1, 0, 2, 3, 4),
                                         vc.transpose(1, 0, 2, 3, 4)))
    o = (o_intra + o_inter.transpose(1, 0, 2, 3, 4)).reshape(Bn, S, RET_HEADS, RET_HEAD_V)
    mu = jnp.mean(o, axis=-1, keepdims=True)
    oc = o - mu
    y = oc * lax.rsqrt(jnp.mean(oc * oc, axis=-1, keepdims=True) + EPS)
    y = y * norm_w.astype(f32).reshape(RET_HEADS, RET_HEAD_V)
    y = y.reshape(Bn, S, RET_V_DIM) * jax.nn.silu(g.astype(f32))
    return y.astype(q.dtype)


def setup_inputs(seed: int = 0) -> dict:
    key = jax.random.key(seed)
    ks = jax.random.split(key, 16)
    f32 = jnp.float32

    def dense(k, shape, fan_in):
        return jax.random.normal(k, shape, f32) * (fan_in ** -0.5)

    def gain(k, shape):
        return 1.0 + 0.02 * jax.random.normal(k, shape, f32)

    return {
        "x": jax.random.normal(ks[0], (BATCH, SEQ, D_MODEL), f32),
        "positions": jnp.broadcast_to(jnp.arange(SEQ, dtype=jnp.int32), (BATCH, SEQ)),
        "norm_mix_w": gain(ks[1], (DEPTH, D_MODEL)),
        "norm_ffn_w": gain(ks[2], (DEPTH, D_MODEL)),
        "norm_final_w": gain(ks[3], (D_MODEL,)),
        "even_w_in": dense(ks[4], (N_EVEN, D_MODEL, EVEN_IN), D_MODEL),
        "conv_w": dense(ks[5], (N_EVEN, CONV_WIDTH, CONV_DIM), CONV_WIDTH),
        "hgrn_lb_logits": 0.1 * jax.random.normal(ks[6], (N_EVEN + 1, HGRN_KEY_DIM), f32),
        "hgrn_norm_w": gain(ks[7], (N_EVEN, HGRN_VAL_DIM)),
        "even_w_out": dense(ks[8], (N_EVEN, EVEN_MIX, D_MODEL), EVEN_MIX),
        "odd_w_in": dense(ks[9], (N_ODD, D_MODEL, ODD_IN), D_MODEL),
        "ret_norm_w": gain(ks[10], (N_ODD, RET_V_DIM)),
        "odd_w_out": dense(ks[11], (N_ODD, RET_V_DIM, D_MODEL), RET_V_DIM),
        "ffn_w_in": dense(ks[12], (DEPTH, D_MODEL, 2 * FFN_HIDDEN), D_MODEL),
        "ffn_w_out": dense(ks[13], (DEPTH, FFN_HIDDEN, D_MODEL), FFN_HIDDEN),
    }


def reference(x, positions, norm_mix_w, norm_ffn_w, norm_final_w, even_w_in, conv_w,
              hgrn_lb_logits, hgrn_norm_w, even_w_out, odd_w_in, ret_norm_w, odd_w_out,
              ffn_w_in, ffn_w_out):
    lbs = jnp.cumsum(jax.nn.softmax(hgrn_lb_logits.astype(jnp.float32), axis=0), axis=0)[:N_EVEN]
    even_pts = _split_points(EVEN_SPLITS)
    odd_pts = _split_points(ODD_SPLITS)
    h = x
    for layer in range(DEPTH):
        hn = rmsnorm(h, norm_mix_w[layer])
        if layer % 2 == 0:
            e = layer // 2
            proj = hn @ even_w_in[e]
            a, b, c, q, f, i, g = jnp.split(proj, even_pts, axis=-1)
            y_conv = short_conv_mixer(a, b, c, conv_w[e])
            y_rec = hgrn2_mixer(q, f, i, g, lbs[e], hgrn_norm_w[e])
            mix = jnp.concatenate([y_conv, y_rec], axis=-1) @ even_w_out[e]
        else:
            o = layer // 2
            proj = hn @ odd_w_in[o]
            q, k, v, g = jnp.split(proj, odd_pts, axis=-1)
            mix = retention_mixer(q, k, v, g, positions, ret_norm_w[o]) @ odd_w_out[o]
        h = h + mix
        hn = rmsnorm(h, norm_ffn_w[layer])
        gate, up = jnp.split(hn @ ffn_w_in[layer], 2, axis=-1)
        h = h + (jax.nn.silu(gate) * up) @ ffn_w_out[layer]
    return rmsnorm(h, norm_final_w)
```

```python
import functools

import numpy as np
import jax
import jax.numpy as jnp
from jax import lax
from jax.experimental import pallas as pl
from jax.experimental.pallas import tpu as pltpu

F32 = jnp.float32
BF16 = jnp.bfloat16

D_MODEL = 1024
EPS = 1e-6

CONV_DIM = 512
CONV_WIDTH = 3
HGRN_HEADS = 4
HGRN_DK = 128
HGRN_DV = 128
HGRN_DIM = HGRN_HEADS * HGRN_DK
EVEN_IN = 3 * CONV_DIM + 4 * HGRN_DIM
EVEN_MIX = CONV_DIM + HGRN_DIM
HGRN_CHUNK = 64

RET_HEADS = 4
RET_DK = 256
RET_DV = 512
RET_K_DIM = RET_HEADS * RET_DK
RET_V_DIM = RET_HEADS * RET_DV
ODD_IN = 2 * RET_K_DIM + 2 * RET_V_DIM
RET_CHUNK = 128
ROPE_BASE = 10000.0

FFN_HIDDEN = 2816

V7X_VMEM_BYTES = 64 * 1024 * 1024
SUBLANES = 8

TM_MIXER = 256
TM_FFN = 256

_NT = (((1,), (1,)), ((), ()))
_TN = (((0,), (0,)), ((), ()))


def _vmem_limit(resident_bytes):
    return int(min(V7X_VMEM_BYTES - 4 * 1024 * 1024, resident_bytes + 24 * 1024 * 1024))


def _resident(shape):
    zeros = (0,) * len(shape)
    return pl.BlockSpec(shape, lambda *_: zeros, pipeline_mode=pl.Buffered(1))


def _rmsnorm(x, w):
    return x * lax.rsqrt(jnp.mean(x * x, axis=-1, keepdims=True) + EPS) * w


def _sigmoid_pair(x):
    e = jnp.exp(-jnp.abs(x))
    r = 1.0 / (1.0 + e)
    er = e * r
    pos = x >= 0
    return jnp.where(pos, r, er), jnp.where(pos, er, r)


def _silu(x):
    return x * _sigmoid_pair(x)[0]


def _hgrn_levels(chunk):
    s, out = chunk // 2, []
    while s >= 1:
        out.append(s)
        s //= 2
    return tuple(out)


def _hgrn_segment_matrix(chunk):
    i = np.arange(chunk)[:, None]
    t = np.arange(chunk)[None, :]
    mats = [t <= i]
    for s in _hgrn_levels(chunk):
        r = (i // (2 * s)) * (2 * s) + s - 1
        upper = (i % (2 * s)) >= s
        mats.append(np.where(upper, (t > r) & (t <= i), (t > i) & (t <= r)))
    return np.concatenate(mats, axis=0).astype(np.float32)


def _split3_bf16(x):
    hi = x.astype(BF16)
    r1 = x - hi.astype(F32)
    mid = r1.astype(BF16)
    lo = (r1 - mid.astype(F32)).astype(BF16)
    return hi, mid, lo


def _even_mixer_kernel(h_ref, nw_ref, win_ref, convw_ref, lb_ref, hnw_ref, wout_ref, seg_ref, o_ref,
                       state_ref, ubuf_ref, proj_ref, ymix_ref, *, tm):
    C = HGRN_CHUNK
    levels = _hgrn_levels(C)

    @pl.when(pl.program_id(1) == 0)
    def _():
        state_ref[...] = jnp.zeros_like(state_ref)
        ubuf_ref[0:SUBLANES, :] = jnp.zeros((SUBLANES, CONV_DIM), F32)

    x = h_ref[0]
    hn = _rmsnorm(x, nw_ref[...]).astype(BF16)
    proj_ref[...] = jnp.dot(hn, win_ref[...], preferred_element_type=F32)

    a = proj_ref[:, 0:CONV_DIM]
    bgate = proj_ref[:, CONV_DIM:2 * CONV_DIM]
    c = proj_ref[:, 2 * CONV_DIM:3 * CONV_DIM]
    u = c * a
    ubuf_ref[SUBLANES:SUBLANES + tm, :] = u
    u1 = ubuf_ref[pl.ds(SUBLANES - 1, tm), :]
    u2 = ubuf_ref[pl.ds(SUBLANES - 2, tm), :]
    cw = convw_ref[...]
    yconv = bgate * (u2 * cw[0:1, :] + u1 * cw[1:2, :] + u * cw[2:3, :])
    ymix_ref[:, 0:CONV_DIM] = yconv.astype(BF16)
    ubuf_ref[0:SUBLANES, :] = ubuf_ref[tm:tm + SUBLANES, :]

    ri = lax.broadcasted_iota(jnp.int32, (C, C), 0)
    cj = lax.broadcasted_iota(jnp.int32, (C, C), 1)
    xor = ri ^ cj
    lower = ri > cj
    level_masks = [lower & (xor >= s) & (xor < 2 * s) for s in levels]
    eye = ri == cj

    lb = lb_ref[...]
    one_m_lb = 1.0 - lb
    seg = seg_ref[...]
    qo = 3 * CONV_DIM
    for ci in range(tm // C):
        rows = slice(ci * C, (ci + 1) * C)
        qx = proj_ref[rows, qo:qo + HGRN_DIM]
        fx = proj_ref[rows, qo + HGRN_DIM:qo + 2 * HGRN_DIM]
        iv = proj_ref[rows, qo + 2 * HGRN_DIM:qo + 3 * HGRN_DIM]
        gx = proj_ref[rows, qo + 3 * HGRN_DIM:qo + 4 * HGRN_DIM]
        qf = _silu(qx)
        sig, nsig = _sigmoid_pair(fx)
        logf = jnp.log(lb + one_m_lb * sig)
        kf = one_m_lb * nsig
        hi, mid, lo = _split3_bf16(logf)
        segsum = (jnp.dot(seg, hi, preferred_element_type=F32)
                  + jnp.dot(seg, mid, preferred_element_type=F32)
                  + jnp.dot(seg, lo, preferred_element_type=F32))
        bcum = segsum[0:C]
        blast = bcum[C - 1:C, :]
        qdec = (qf * jnp.exp(bcum)).astype(BF16)
        kdec = (kf * jnp.exp(blast - bcum)).astype(BF16)
        sdec = jnp.exp(blast)
        vb = iv.astype(BF16)
        qk = qf * kf
        gate = _silu(gx)
        qs, ks = [], []
        for li in range(len(levels)):
            w = jnp.exp(segsum[(li + 1) * C:(li + 2) * C])
            qs.append((qf * w).astype(BF16))
            ks.append((kf * w).astype(BF16))
        for hd in range(HGRN_HEADS):
            sl = slice(hd * HGRN_DK, (hd + 1) * HGRN_DK)
            scores = jnp.zeros((C, C), F32)
            for li in range(len(levels)):
                p = lax.dot_general(qs[li][:, sl], ks[li][:, sl], _NT, preferred_element_type=F32)
                scores = jnp.where(level_masks[li], p, scores)
            diag = jnp.sum(qk[:, sl], axis=-1, keepdims=True)
            scores = jnp.where(eye, diag, scores)
            st = state_ref[hd]
            o = (jnp.dot(scores.astype(BF16), vb[:, sl], preferred_element_type=F32)
                 + lax.dot_general(qdec[:, sl], st.astype(BF16), _NT, preferred_element_type=F32))
            state_ref[hd] = sdec[:, sl] * st + lax.dot_general(vb[:, sl], kdec[:, sl], _TN,
                                                                preferred_element_type=F32)
            y = _rmsnorm(o, hnw_ref[:, sl]) * gate[:, sl]
            ymix_ref[rows, CONV_DIM + hd * HGRN_DV:CONV_DIM + (hd + 1) * HGRN_DV] = y.astype(BF16)

    o_ref[0] = x + jnp.dot(ymix_ref[...], wout_ref[...], preferred_element_type=F32)


def _even_mixer(h, norm_w, w_in, conv_w, lb, hgrn_norm_w, w_out):
    B, S, D = h.shape
    tm = TM_MIXER
    seg = jnp.asarray(_hgrn_segment_matrix(HGRN_CHUNK), dtype=BF16)
    resident = (w_in.size + w_out.size) * 2 + tm * EVEN_IN * 4 + 4 * tm * D * 4
    return pl.pallas_call(
        functools.partial(_even_mixer_kernel, tm=tm),
        grid=(B, S // tm),
        in_specs=[
            pl.BlockSpec((1, tm, D), lambda b, t: (b, t, 0)),
            _resident((1, D)),
            _resident((D, EVEN_IN)),
            _resident((CONV_WIDTH, CONV_DIM)),
            _resident((1, HGRN_DIM)),
            _resident((1, HGRN_DIM)),
            _resident((EVEN_MIX, D)),
            _resident(seg.shape),
        ],
        out_specs=pl.BlockSpec((1, tm, D), lambda b, t: (b, t, 0)),
        out_shape=jax.ShapeDtypeStruct((B, S, D), F32),
        scratch_shapes=[
            pltpu.VMEM((HGRN_HEADS, HGRN_DV, HGRN_DK), F32),
            pltpu.VMEM((tm + SUBLANES, CONV_DIM), F32),
            pltpu.VMEM((tm, EVEN_IN), F32),
            pltpu.VMEM((tm, EVEN_MIX), BF16),
        ],
        compiler_params=pltpu.CompilerParams(
            dimension_semantics=("arbitrary", "arbitrary"),
            vmem_limit_bytes=_vmem_limit(resident)),
        name="even_mixer",
    )(h, norm_w, w_in, conv_w, lb, hgrn_norm_w, w_out, seg)


def _odd_mixer_kernel(cdec_ref, h_ref, pos_ref, nw_ref, win_ref, invf_ref, dmat_ref, qdec_ref, kdec_ref,
                      gnw_ref, wout_ref, o_ref, state_ref, proj_ref, ymix_ref, *, tm):
    C = RET_CHUNK
    half = RET_DK // 2

    @pl.when(pl.program_id(1) == 0)
    def _():
        state_ref[...] = jnp.zeros_like(state_ref)

    x = h_ref[0]
    hn = _rmsnorm(x, nw_ref[...]).astype(BF16)
    proj_ref[...] = jnp.dot(hn, win_ref[...], preferred_element_type=F32)

    ang = pos_ref[0] * invf_ref[...]
    cos = jnp.cos(ang)
    sin = jnp.sin(ang)

    def rotate(col0, rows, cs, sn):
        x1 = proj_ref[rows, col0:col0 + half]
        x2 = proj_ref[rows, col0 + half:col0 + 2 * half]
        return jnp.concatenate([x1 * cs - x2 * sn, x1 * sn + x2 * cs], axis=-1)

    for ci in range(tm // C):
        rows = slice(ci * C, (ci + 1) * C)
        cs = cos[ci * C:(ci + 1) * C]
        sn = sin[ci * C:(ci + 1) * C]
        for hd in range(RET_HEADS):
            q = rotate(hd * RET_DK, rows, cs, sn)
            k = rotate(RET_K_DIM + hd * RET_DK, rows, cs, sn) * (RET_DK ** -0.5)
            vo = 2 * RET_K_DIM + hd * RET_DV
            go = 2 * RET_K_DIM + RET_V_DIM + hd * RET_DV
            v = proj_ref[rows, vo:vo + RET_DV].astype(BF16)
            g = proj_ref[rows, go:go + RET_DV]
            scores = lax.dot_general(q.astype(BF16), k.astype(BF16), _NT,
                                     preferred_element_type=F32) * dmat_ref[hd]
            st = state_ref[hd]
            o = (jnp.dot(scores.astype(BF16), v, preferred_element_type=F32)
                 + jnp.dot((q * qdec_ref[hd]).astype(BF16), st.astype(BF16), preferred_element_type=F32))
            state_ref[hd] = cdec_ref[hd] * st + lax.dot_general((k * kdec_ref[hd]).astype(BF16), v, _TN,
                                                                preferred_element_type=F32)
            oc = o - jnp.mean(o, axis=-1, keepdims=True)
            y = oc * lax.rsqrt(jnp.mean(oc * oc, axis=-1, keepdims=True) + EPS)
            y = y * gnw_ref[:, hd * RET_DV:(hd + 1) * RET_DV] * _silu(g)
            ymix_ref[rows, hd * RET_DV:(hd + 1) * RET_DV] = y.astype(BF16)

    o_ref[0] = x + jnp.dot(ymix_ref[...], wout_ref[...], preferred_element_type=F32)


def _retention_tables():
    C = RET_CHUNK
    log_gamma = jnp.log1p(-jnp.exp2(-5.0 - jnp.arange(RET_HEADS, dtype=F32)))
    idx = jnp.arange(C, dtype=F32)
    rel = idx[:, None] - idx[None, :]
    dmat = jnp.where(rel >= 0, jnp.exp(log_gamma[:, None, None] * jnp.maximum(rel, 0.0)), 0.0)
    qdec = jnp.exp(log_gamma[:, None] * (idx[None, :] + 1.0))
    kdec = jnp.exp(log_gamma[:, None] * (C - 1.0 - idx[None, :]))
    cdec = jnp.exp(log_gamma * C)
    widen = lambda t: jnp.broadcast_to(t[:, :, None], (RET_HEADS, C, RET_DK))
    return dmat, widen(qdec), widen(kdec), cdec


def _odd_mixer(h, pos, norm_w, w_in, gn_w, w_out):
    B, S, D = h.shape
    tm = TM_MIXER
    half = RET_DK // 2
    inv_freq = (ROPE_BASE ** (-jnp.arange(half, dtype=F32) / half)).reshape(1, half)
    dmat, qdec, kdec, cdec = _retention_tables()
    resident = ((w_in.size + w_out.size) * 2 + tm * ODD_IN * 4 + 4 * tm * D * 4
                + RET_HEADS * RET_DK * RET_DV * 4 + tm * RET_V_DIM * 2)
    return pl.pallas_call(
        functools.partial(_odd_mixer_kernel, tm=tm),
        grid=(B, S // tm),
        in_specs=[
            pl.BlockSpec(memory_space=pltpu.SMEM),
            pl.BlockSpec((1, tm, D), lambda b, t: (b, t, 0)),
            pl.BlockSpec((1, tm, 1), lambda b, t: (b, t, 0)),
            _resident((1, D)),
            _resident((D, ODD_IN)),
            _resident((1, half)),
            _resident(dmat.shape),
            _resident(qdec.shape),
            _resident(kdec.shape),
            _resident((1, RET_V_DIM)),
            _resident((RET_V_DIM, D)),
        ],
        out_specs=pl.BlockSpec((1, tm, D), lambda b, t: (b, t, 0)),
        out_shape=jax.ShapeDtypeStruct((B, S, D), F32),
        scratch_shapes=[
            pltpu.VMEM((RET_HEADS, RET_DK, RET_DV), F32),
            pltpu.VMEM((tm, ODD_IN), F32),
            pltpu.VMEM((tm, RET_V_DIM), BF16),
        ],
        compiler_params=pltpu.CompilerParams(
            dimension_semantics=("arbitrary", "arbitrary"),
            vmem_limit_bytes=_vmem_limit(resident)),
        name="odd_mixer",
    )(cdec, h, pos, norm_w, w_in, inv_freq, dmat, qdec, kdec, gn_w, w_out)


def _ffn_kernel(h_ref, nw_ref, win_ref, wout_ref, fnw_ref, o_ref, *, final_norm):
    x = h_ref[...]
    hn = _rmsnorm(x, nw_ref[...]).astype(BF16)
    gu = jnp.dot(hn, win_ref[...], preferred_element_type=F32)
    act = (_silu(gu[:, :FFN_HIDDEN]) * gu[:, FFN_HIDDEN:]).astype(BF16)
    y = x + jnp.dot(act, wout_ref[...], preferred_element_type=F32)
    if final_norm:
        y = _rmsnorm(y, fnw_ref[...])
    o_ref[...] = y


def _ffn(h2d, norm_w, w_in, w_out, final_w, final_norm):
    T, D = h2d.shape
    tm = TM_FFN
    resident = (w_in.size + w_out.size) * 2 + 4 * tm * D * 4 + tm * 2 * FFN_HIDDEN * 4
    return pl.pallas_call(
        functools.partial(_ffn_kernel, final_norm=final_norm),
        grid=(T // tm,),
        in_specs=[
            pl.BlockSpec((tm, D), lambda i: (i, 0)),
            _resident((1, D)),
            _resident((D, 2 * FFN_HIDDEN)),
            _resident((FFN_HIDDEN, D)),
            _resident((1, D)),
        ],
        out_specs=pl.BlockSpec((tm, D), lambda i: (i, 0)),
        out_shape=jax.ShapeDtypeStruct((T, D), F32),
        compiler_params=pltpu.CompilerParams(
            dimension_semantics=("arbitrary",),
            vmem_limit_bytes=_vmem_limit(resident)),
        name="ffn_final" if final_norm else "ffn",
    )(h2d, norm_w, w_in, w_out, final_w)


def kernel(x, positions, norm_mix_w, norm_ffn_w, norm_final_w, even_w_in, conv_w, hgrn_lb_logits, hgrn_norm_w,
           even_w_out, odd_w_in, ret_norm_w, odd_w_out, ffn_w_in, ffn_w_out):
    B, S, D = x.shape
    assert D == D_MODEL and S % TM_MIXER == 0 and (B * S) % TM_FFN == 0
    assert norm_mix_w.shape[0] == 2 and even_w_in.shape[0] == 1 and odd_w_in.shape[0] == 1
    lbs = jnp.cumsum(jax.nn.softmax(hgrn_lb_logits.astype(F32), axis=0), axis=0)
    pos = positions.astype(F32).reshape(B, S, 1)
    row = lambda w: w.reshape(1, -1).astype(F32)

    h = _even_mixer(x, row(norm_mix_w[0]), even_w_in[0].astype(BF16), conv_w[0], row(lbs[0]),
                    row(hgrn_norm_w[0]), even_w_out[0].astype(BF16))
    h = _ffn(h.reshape(B * S, D), row(norm_ffn_w[0]), ffn_w_in[0].astype(BF16), ffn_w_out[0].astype(BF16),
             row(norm_final_w), False).reshape(B, S, D)
    h = _odd_mixer(h, pos, row(norm_mix_w[1]), odd_w_in[0].astype(BF16), row(ret_norm_w[0]),
                   odd_w_out[0].astype(BF16))
    h = _ffn(h.reshape(B * S, D), row(norm_ffn_w[1]), ffn_w_in[1].astype(BF16), ffn_w_out[1].astype(BF16),
             row(norm_final_w), True)
    return h.reshape(B, S, D)
```

```python
import functools

import numpy as np
import jax
import jax.numpy as jnp
from jax import lax
from jax.experimental import pallas as pl
from jax.experimental.pallas import tpu as pltpu

F32 = jnp.float32
BF16 = jnp.bfloat16

D_MODEL = 1024
EPS = 1e-6

CONV_DIM = 512
CONV_WIDTH = 3
HGRN_HEADS = 4
HGRN_DK = 128
HGRN_DV = 128
HGRN_DIM = HGRN_HEADS * HGRN_DK
EVEN_IN = 3 * CONV_DIM + 4 * HGRN_DIM
EVEN_MIX = CONV_DIM + HGRN_DIM
HGRN_CHUNK = 64

RET_HEADS = 4
RET_DK = 256
RET_DV = 512
RET_K_DIM = RET_HEADS * RET_DK
RET_V_DIM = RET_HEADS * RET_DV
ODD_IN = 2 * RET_K_DIM + 2 * RET_V_DIM
RET_CHUNK = 128
ROPE_BASE = 10000.0

FFN_HIDDEN = 2816

V7X_VMEM_BYTES = 64 * 1024 * 1024
SUBLANES = 8

TM_MIXER = 256
TM_FFN = 256

_NT = (((1,), (1,)), ((), ()))
_TN = (((0,), (0,)), ((), ()))


def _vmem_limit(resident_bytes):
    return int(min(V7X_VMEM_BYTES - 4 * 1024 * 1024, resident_bytes + 24 * 1024 * 1024))


def _resident(shape):
    zeros = (0,) * len(shape)
    return pl.BlockSpec(shape, lambda *_: zeros, pipeline_mode=pl.Buffered(1))


def _rmsnorm(x, w):
    return x * lax.rsqrt(jnp.mean(x * x, axis=-1, keepdims=True) + EPS) * w


def _sigmoid_pair(x):
    e = jnp.exp(-jnp.abs(x))
    r = 1.0 / (1.0 + e)
    er = e * r
    pos = x >= 0
    return jnp.where(pos, r, er), jnp.where(pos, er, r)


def _silu(x):
    return x * _sigmoid_pair(x)[0]


def _hgrn_levels(chunk):
    s, out = chunk // 2, []
    while s >= 1:
        out.append(s)
        s //= 2
    return tuple(out)


def _hgrn_reference_rows(b3):
    G = b3.shape[0]
    tail = b3.shape[1:]
    last = jnp.broadcast_to(b3[:, SUBLANES - 1:SUBLANES, :], b3.shape)
    sub = lax.broadcasted_iota(jnp.int32, b3.shape, 1)
    out = []
    for s in _hgrn_levels(G * SUBLANES):
        if s >= SUBLANES:
            per = 2 * s // SUBLANES
            src = last.reshape((G // per, per) + tail)[:, per // 2 - 1:per // 2]
            out.append(jnp.broadcast_to(src, (G // per, per) + tail).reshape(b3.shape))
        elif s == 4:
            out.append(jnp.broadcast_to(b3[:, 3:4, :], b3.shape))
        elif s == 2:
            pos = sub & 3
            nxt1 = pltpu.roll(b3, SUBLANES - 1, 1)
            prev1 = pltpu.roll(b3, 1, 1)
            prev2 = pltpu.roll(b3, 2, 1)
            out.append(jnp.where(pos == 0, nxt1, jnp.where(pos == 1, b3, jnp.where(pos == 2, prev1, prev2))))
        else:
            out.append(jnp.where((sub & 1) == 1, pltpu.roll(b3, 1, 1), b3))
    return out


def _split3_bf16(x):
    hi = x.astype(BF16)
    r1 = x - hi.astype(F32)
    mid = r1.astype(BF16)
    lo = (r1 - mid.astype(F32)).astype(BF16)
    return hi, mid, lo


def _even_mixer_kernel(h_ref, nw_ref, win_ref, convw_ref, lb_ref, hnw_ref, wout_ref, o_ref,
                       state_ref, ubuf_ref, proj_ref, ymix_ref, *, tm):
    C = HGRN_CHUNK
    levels = _hgrn_levels(C)

    @pl.when(pl.program_id(1) == 0)
    def _():
        state_ref[...] = jnp.zeros_like(state_ref)
        ubuf_ref[0:SUBLANES, :] = jnp.zeros((SUBLANES, CONV_DIM), F32)

    x = h_ref[0]
    hn = _rmsnorm(x, nw_ref[...]).astype(BF16)
    proj_ref[...] = jnp.dot(hn, win_ref[...], preferred_element_type=F32)

    a = proj_ref[:, 0:CONV_DIM]
    bgate = proj_ref[:, CONV_DIM:2 * CONV_DIM]
    c = proj_ref[:, 2 * CONV_DIM:3 * CONV_DIM]
    u = c * a
    ubuf_ref[SUBLANES:SUBLANES + tm, :] = u
    u1 = ubuf_ref[pl.ds(SUBLANES - 1, tm), :]
    u2 = ubuf_ref[pl.ds(SUBLANES - 2, tm), :]
    cw = convw_ref[...]
    yconv = bgate * (u2 * cw[0:1, :] + u1 * cw[1:2, :] + u * cw[2:3, :])
    ymix_ref[:, 0:CONV_DIM] = yconv.astype(BF16)
    ubuf_ref[0:SUBLANES, :] = ubuf_ref[tm:tm + SUBLANES, :]

    ri = lax.broadcasted_iota(jnp.int32, (C, C), 0)
    cj = lax.broadcasted_iota(jnp.int32, (C, C), 1)
    xor = ri ^ cj
    lower = ri > cj
    level_masks = [lower & (xor >= s) & (xor < 2 * s) for s in levels]
    eye = ri == cj

    lb = lb_ref[...]
    one_m_lb = 1.0 - lb
    tri = (ri >= cj).astype(BF16)
    qo = 3 * CONV_DIM
    for ci in range(tm // C):
        rows = slice(ci * C, (ci + 1) * C)
        qx = proj_ref[rows, qo:qo + HGRN_DIM]
        fx = proj_ref[rows, qo + HGRN_DIM:qo + 2 * HGRN_DIM]
        iv = proj_ref[rows, qo + 2 * HGRN_DIM:qo + 3 * HGRN_DIM]
        gx = proj_ref[rows, qo + 3 * HGRN_DIM:qo + 4 * HGRN_DIM]
        qf = _silu(qx)
        sig, nsig = _sigmoid_pair(fx)
        logf = jnp.log(lb + one_m_lb * sig)
        kf = one_m_lb * nsig
        hi, mid, lo = _split3_bf16(logf)
        bcum = (jnp.dot(tri, hi, preferred_element_type=F32)
                + jnp.dot(tri, mid, preferred_element_type=F32)
                + jnp.dot(tri, lo, preferred_element_type=F32))
        b3 = bcum.reshape(C // SUBLANES, SUBLANES, HGRN_DIM)
        level_w = [jnp.exp(-jnp.abs(b3 - ref)).reshape(C, HGRN_DIM) for ref in _hgrn_reference_rows(b3)]
        blast = bcum[C - 1:C, :]
        qdec = (qf * jnp.exp(bcum)).astype(BF16)
        kdec = (kf * jnp.exp(blast - bcum)).astype(BF16)
        sdec = jnp.exp(blast)
        vb = iv.astype(BF16)
        qk = qf * kf
        gate = _silu(gx)
        qs, ks = [], []
        for li in range(len(levels)):
            qs.append((qf * level_w[li]).astype(BF16))
            ks.append((kf * level_w[li]).astype(BF16))
        for hd in range(HGRN_HEADS):
            sl = slice(hd * HGRN_DK, (hd + 1) * HGRN_DK)
            scores = jnp.zeros((C, C), F32)
            for li in range(len(levels)):
                p = lax.dot_general(qs[li][:, sl], ks[li][:, sl], _NT, preferred_element_type=F32)
                scores = jnp.where(level_masks[li], p, scores)
            diag = jnp.sum(qk[:, sl], axis=-1, keepdims=True)
            scores = jnp.where(eye, diag, scores)
            st = state_ref[hd]
            o = (jnp.dot(scores.astype(BF16), vb[:, sl], preferred_element_type=F32)
                 + lax.dot_general(qdec[:, sl], st.astype(BF16), _NT, preferred_element_type=F32))
            state_ref[hd] = sdec[:, sl] * st + lax.dot_general(vb[:, sl], kdec[:, sl], _TN,
                                                                preferred_element_type=F32)
            y = _rmsnorm(o, hnw_ref[:, sl]) * gate[:, sl]
            ymix_ref[rows, CONV_DIM + hd * HGRN_DV:CONV_DIM + (hd + 1) * HGRN_DV] = y.astype(BF16)

    o_ref[0] = x + jnp.dot(ymix_ref[...], wout_ref[...], preferred_element_type=F32)


def _even_mixer(h, norm_w, w_in, conv_w, lb, hgrn_norm_w, w_out):
    B, S, D = h.shape
    tm = TM_MIXER
    resident = (w_in.size + w_out.size) * 2 + tm * EVEN_IN * 4 + 4 * tm * D * 4
    return pl.pallas_call(
        functools.partial(_even_mixer_kernel, tm=tm),
        grid=(B, S // tm),
        in_specs=[
            pl.BlockSpec((1, tm, D), lambda b, t: (b, t, 0)),
            _resident((1, D)),
            _resident((D, EVEN_IN)),
            _resident((CONV_WIDTH, CONV_DIM)),
            _resident((1, HGRN_DIM)),
            _resident((1, HGRN_DIM)),
            _resident((EVEN_MIX, D)),
        ],
        out_specs=pl.BlockSpec((1, tm, D), lambda b, t: (b, t, 0)),
        out_shape=jax.ShapeDtypeStruct((B, S, D), F32),
        scratch_shapes=[
            pltpu.VMEM((HGRN_HEADS, HGRN_DV, HGRN_DK), F32),
            pltpu.VMEM((tm + SUBLANES, CONV_DIM), F32),
            pltpu.VMEM((tm, EVEN_IN), F32),
            pltpu.VMEM((tm, EVEN_MIX), BF16),
        ],
        compiler_params=pltpu.CompilerParams(
            dimension_semantics=("arbitrary", "arbitrary"),
            vmem_limit_bytes=_vmem_limit(resident)),
        name="even_mixer",
    )(h, norm_w, w_in, conv_w, lb, hgrn_norm_w, w_out)


def _odd_mixer_kernel(cdec_ref, h_ref, pos_ref, nw_ref, win_ref, invf_ref, dmat_ref, qdec_ref, kdec_ref,
                      gnw_ref, wout_ref, o_ref, state_ref, proj_ref, ymix_ref, *, tm):
    C = RET_CHUNK
    half = RET_DK // 2

    @pl.when(pl.program_id(1) == 0)
    def _():
        state_ref[...] = jnp.zeros_like(state_ref)

    x = h_ref[0]
    hn = _rmsnorm(x, nw_ref[...]).astype(BF16)
    proj_ref[...] = jnp.dot(hn, win_ref[...], preferred_element_type=F32)

    ang = pos_ref[0] * invf_ref[...]
    cos = jnp.cos(ang)
    sin = jnp.sin(ang)

    def rotate(col0, rows, cs, sn):
        x1 = proj_ref[rows, col0:col0 + half]
        x2 = proj_ref[rows, col0 + half:col0 + 2 * half]
        return jnp.concatenate([x1 * cs - x2 * sn, x1 * sn + x2 * cs], axis=-1)

    for ci in range(tm // C):
        rows = slice(ci * C, (ci + 1) * C)
        cs = cos[ci * C:(ci + 1) * C]
        sn = sin[ci * C:(ci + 1) * C]
        for hd in range(RET_HEADS):
            q = rotate(hd * RET_DK, rows, cs, sn)
            k = rotate(RET_K_DIM + hd * RET_DK, rows, cs, sn) * (RET_DK ** -0.5)
            vo = 2 * RET_K_DIM + hd * RET_DV
            go = 2 * RET_K_DIM + RET_V_DIM + hd * RET_DV
            v = proj_ref[rows, vo:vo + RET_DV].astype(BF16)
            g = proj_ref[rows, go:go + RET_DV]
            scores = lax.dot_general(q.astype(BF16), k.astype(BF16), _NT,
                                     preferred_element_type=F32) * dmat_ref[hd]
            st = state_ref[hd]
            o = (jnp.dot(scores.astype(BF16), v, preferred_element_type=F32)
                 + jnp.dot((q * qdec_ref[hd]).astype(BF16), st.astype(BF16), preferred_element_type=F32))
            state_ref[hd] = cdec_ref[hd] * st + lax.dot_general((k * kdec_ref[hd]).astype(BF16), v, _TN,
                                                                preferred_element_type=F32)
            oc = o - jnp.mean(o, axis=-1, keepdims=True)
            y = oc * lax.rsqrt(jnp.mean(oc * oc, axis=-1, keepdims=True) + EPS)
            y = y * gnw_ref[:, hd * RET_DV:(hd + 1) * RET_DV] * _silu(g)
            ymix_ref[rows, hd * RET_DV:(hd + 1) * RET_DV] = y.astype(BF16)

    o_ref[0] = x + jnp.dot(ymix_ref[...], wout_ref[...], preferred_element_type=F32)


def _retention_tables():
    C = RET_CHUNK
    log_gamma = jnp.log1p(-jnp.exp2(-5.0 - jnp.arange(RET_HEADS, dtype=F32)))
    idx = jnp.arange(C, dtype=F32)
    rel = idx[:, None] - idx[None, :]
    dmat = jnp.where(rel >= 0, jnp.exp(log_gamma[:, None, None] * jnp.maximum(rel, 0.0)), 0.0)
    qdec = jnp.exp(log_gamma[:, None] * (idx[None, :] + 1.0))
    kdec = jnp.exp(log_gamma[:, None] * (C - 1.0 - idx[None, :]))
    cdec = jnp.exp(log_gamma * C)
    widen = lambda t: jnp.broadcast_to(t[:, :, None], (RET_HEADS, C, RET_DK))
    return dmat, widen(qdec), widen(kdec), cdec


def _odd_mixer(h, pos, norm_w, w_in, gn_w, w_out):
    B, S, D = h.shape
    tm = TM_MIXER
    half = RET_DK // 2
    inv_freq = (ROPE_BASE ** (-jnp.arange(half, dtype=F32) / half)).reshape(1, half)
    dmat, qdec, kdec, cdec = _retention_tables()
    resident = ((w_in.size + w_out.size) * 2 + tm * ODD_IN * 4 + 4 * tm * D * 4
                + RET_HEADS * RET_DK * RET_DV * 4 + tm * RET_V_DIM * 2)
    return pl.pallas_call(
        functools.partial(_odd_mixer_kernel, tm=tm),
        grid=(B, S // tm),
        in_specs=[
            pl.BlockSpec(memory_space=pltpu.SMEM),
            pl.BlockSpec((1, tm, D), lambda b, t: (b, t, 0)),
            pl.BlockSpec((1, tm, 1), lambda b, t: (b, t, 0)),
            _resident((1, D)),
            _resident((D, ODD_IN)),
            _resident((1, half)),
            _resident(dmat.shape),
            _resident(qdec.shape),
            _resident(kdec.shape),
            _resident((1, RET_V_DIM)),
            _resident((RET_V_DIM, D)),
        ],
        out_specs=pl.BlockSpec((1, tm, D), lambda b, t: (b, t, 0)),
        out_shape=jax.ShapeDtypeStruct((B, S, D), F32),
        scratch_shapes=[
            pltpu.VMEM((RET_HEADS, RET_DK, RET_DV), F32),
            pltpu.VMEM((tm, ODD_IN), F32),
            pltpu.VMEM((tm, RET_V_DIM), BF16),
        ],
        compiler_params=pltpu.CompilerParams(
            dimension_semantics=("arbitrary", "arbitrary"),
            vmem_limit_bytes=_vmem_limit(resident)),
        name="odd_mixer",
    )(cdec, h, pos, norm_w, w_in, inv_freq, dmat, qdec, kdec, gn_w, w_out)


def _ffn_kernel(h_ref, nw_ref, win_ref, wout_ref, fnw_ref, o_ref, *, final_norm):
    x = h_ref[...]
    hn = _rmsnorm(x, nw_ref[...]).astype(BF16)
    gu = jnp.dot(hn, win_ref[...], preferred_element_type=F32)
    act = (_silu(gu[:, :FFN_HIDDEN]) * gu[:, FFN_HIDDEN:]).astype(BF16)
    y = x + jnp.dot(act, wout_ref[...], preferred_element_type=F32)
    if final_norm:
        y = _rmsnorm(y, fnw_ref[...])
    o_ref[...] = y


def _ffn(h2d, norm_w, w_in, w_out, final_w, final_norm):
    T, D = h2d.shape
    tm = TM_FFN
    resident = (w_in.size + w_out.size) * 2 + 4 * tm * D * 4 + tm * 2 * FFN_HIDDEN * 4
    return pl.pallas_call(
        functools.partial(_ffn_kernel, final_norm=final_norm),
        grid=(T // tm,),
        in_specs=[
            pl.BlockSpec((tm, D), lambda i: (i, 0)),
            _resident((1, D)),
            _resident((D, 2 * FFN_HIDDEN)),
            _resident((FFN_HIDDEN, D)),
            _resident((1, D)),
        ],
        out_specs=pl.BlockSpec((tm, D), lambda i: (i, 0)),
        out_shape=jax.ShapeDtypeStruct((T, D), F32),
        compiler_params=pltpu.CompilerParams(
            dimension_semantics=("arbitrary",),
            vmem_limit_bytes=_vmem_limit(resident)),
        name="ffn_final" if final_norm else "ffn",
    )(h2d, norm_w, w_in, w_out, final_w)


def kernel(x, positions, norm_mix_w, norm_ffn_w, norm_final_w, even_w_in, conv_w, hgrn_lb_logits, hgrn_norm_w,
           even_w_out, odd_w_in, ret_norm_w, odd_w_out, ffn_w_in, ffn_w_out):
    B, S, D = x.shape
    assert D == D_MODEL and S % TM_MIXER == 0 and (B * S) % TM_FFN == 0
    assert norm_mix_w.shape[0] == 2 and even_w_in.shape[0] == 1 and odd_w_in.shape[0] == 1
    lbs = jnp.cumsum(jax.nn.softmax(hgrn_lb_logits.astype(F32), axis=0), axis=0)
    pos = positions.astype(F32).reshape(B, S, 1)
    row = lambda w: w.reshape(1, -1).astype(F32)

    h = _even_mixer(x, row(norm_mix_w[0]), even_w_in[0].astype(BF16), conv_w[0], row(lbs[0]),
                    row(hgrn_norm_w[0]), even_w_out[0].astype(BF16))
    h = _ffn(h.reshape(B * S, D), row(norm_ffn_w[0]), ffn_w_in[0].astype(BF16), ffn_w_out[0].astype(BF16),
             row(norm_final_w), False).reshape(B, S, D)
    h = _odd_mixer(h, pos, row(norm_mix_w[1]), odd_w_in[0].astype(BF16), row(ret_norm_w[0]),
                   odd_w_out[0].astype(BF16))
    h = _ffn(h.reshape(B * S, D), row(norm_ffn_w[1]), ffn_w_in[1].astype(BF16), ffn_w_out[1].astype(BF16),
             row(norm_final_w), True)
    return h.reshape(B, S, D)
```

```python
import functools

import jax
import jax.numpy as jnp
from jax import lax
from jax.experimental import pallas as pl
from jax.experimental.pallas import tpu as pltpu

F32 = jnp.float32
BF16 = jnp.bfloat16

D_MODEL = 1024
EPS = 1e-6
LOG2_E = 1.4426950408889634

CONV_DIM = 512
CONV_WIDTH = 3
HGRN_HEADS = 4
HGRN_DK = 128
HGRN_DV = 128
HGRN_DIM = HGRN_HEADS * HGRN_DK
EVEN_IN = 3 * CONV_DIM + 4 * HGRN_DIM
EVEN_MIX = CONV_DIM + HGRN_DIM
HGRN_CHUNK = 64

RET_HEADS = 4
RET_DK = 256
RET_DV = 512
RET_K_DIM = RET_HEADS * RET_DK
RET_V_DIM = RET_HEADS * RET_DV
ODD_IN = 2 * RET_K_DIM + 2 * RET_V_DIM
RET_CHUNK = 128
ROPE_BASE = 10000.0

FFN_HIDDEN = 2816

V7X_VMEM_BYTES = 64 * 1024 * 1024
SUBLANES = 8
VMEM_RESERVE_BYTES = 4 * 1024 * 1024
TEMPORARIES_BYTES = 24 * 1024 * 1024

TM_EVEN = 1024
TM_ODD = 512
TM_FFN = 1024
FFN_GROUP = 256
ODD_PIECE = 128
EVEN_PIECE = 128
PIECE_BUFFERS = 2
PROJ_GROUP = 256
ODD_PROJ_GROUP = 256
OUT_GROUP = 256
EVEN_BG_PATTERN = (3, 3, 3, 3, 3, 3, 0, 0)
ODD_BG_PATTERN = (2, 2, 2, 2, 0, 0, 0, 0)
STAGE_SLOTS = 8

_NN = (((1,), (0,)), ((), ()))
_NT = (((1,), (1,)), ((), ()))
_TN = (((0,), (0,)), ((), ()))


def _vmem_limit(resident_bytes):
    return int(min(V7X_VMEM_BYTES - VMEM_RESERVE_BYTES, resident_bytes + TEMPORARIES_BYTES))


def _resident(shape):
    zeros = (0,) * len(shape)
    return pl.BlockSpec(shape, lambda *_: zeros, pipeline_mode=pl.Buffered(1))


class _Background:
    def __init__(self):
        self._steps = []

    def add(self, steps):
        self._steps.extend(steps)

    def drain(self, n=None):
        for _ in range(len(self._steps) if n is None else min(n, len(self._steps))):
            self._steps.pop(0)()


class _StagedWeight:
    def __init__(self, rows, cols, chunk):
        assert rows % chunk == 0
        self.rows, self.cols, self.chunk = rows, cols, chunk

    def vmem_bytes(self):
        return self.rows * self.cols * 2 + STAGE_SLOTS * self.chunk * self.cols * 4

    def scratch(self):
        return [pltpu.VMEM((self.rows, self.cols), BF16),
                pltpu.VMEM((STAGE_SLOTS, self.chunk, self.cols), F32),
                pltpu.SemaphoreType.DMA((STAGE_SLOTS,))]

    def load(self, src_hbm, dst_ref, stage_ref, sem_ref):
        n = self.rows // self.chunk
        ahead = STAGE_SLOTS - 1

        def copy(i):
            return pltpu.make_async_copy(src_hbm.at[pl.ds(i * self.chunk, self.chunk), :],
                                         stage_ref.at[i % STAGE_SLOTS], sem_ref.at[i % STAGE_SLOTS])
        for i in range(min(ahead, n)):
            copy(i).start()
        for i in range(n):
            if i + ahead < n:
                copy(i + ahead).start()
            copy(i).wait()
            dst_ref[i * self.chunk:(i + 1) * self.chunk, :] = stage_ref[i % STAGE_SLOTS].astype(BF16)


def _rmsnorm(x, w):
    return x * lax.rsqrt(jnp.mean(x * x, axis=-1, keepdims=True) + EPS) * w


def _sigmoid_pair(x):
    e = jnp.exp(-jnp.abs(x))
    r = 1.0 / (1.0 + e)
    er = e * r
    pos = x >= 0
    return jnp.where(pos, r, er), jnp.where(pos, er, r)


def _silu(x):
    hx = 0.5 * x
    return hx + hx * jnp.tanh(hx)


def _hgrn_levels(chunk):
    s, out = chunk // 2, []
    while s >= 1:
        out.append(s)
        s //= 2
    return tuple(out)


def _hgrn_level_exponents(b3):
    G = b3.shape[0]
    tail = b3.shape[1:]
    last = jnp.broadcast_to(b3[:, SUBLANES - 1:SUBLANES, :], b3.shape)
    sub = lax.broadcasted_iota(jnp.int32, b3.shape, 1)
    out = []
    for s in _hgrn_levels(G * SUBLANES):
        if s >= SUBLANES:
            per = 2 * s // SUBLANES
            blocks = b3.reshape((G // per, per) + tail)
            ref = last.reshape((G // per, per) + tail)[:, per // 2 - 1:per // 2]
            below = ref - blocks[:, :per // 2]
            above = blocks[:, per // 2:] - ref
            out.append(jnp.concatenate([below, above], axis=1).reshape(b3.shape))
        elif s == 4:
            out.append(-jnp.abs(b3 - b3[:, 3:4, :]))
        elif s == 2:
            pos = sub & 3
            nxt1 = pltpu.roll(b3, SUBLANES - 1, 1)
            prev1 = pltpu.roll(b3, 1, 1)
            prev2 = pltpu.roll(b3, 2, 1)
            ref = jnp.where(pos == 0, nxt1, jnp.where(pos == 1, b3, jnp.where(pos == 2, prev1, prev2)))
            out.append(-jnp.abs(b3 - ref))
        else:
            out.append(-jnp.abs(b3 - jnp.where((sub & 1) == 1, pltpu.roll(b3, 1, 1), b3)))
    return out


def _split2_bf16(x):
    hi = x.astype(BF16)
    mid = (x - hi.astype(F32)).astype(BF16)
    return hi, mid


def _even_mixer_kernel(h_ref, nw_ref, win_hbm, convw_ref, lb_ref, hnw_ref, wout_hbm, o_ref,
                       win_ref, win_stage, win_sem, wout_ref, wout_stage, wout_sem,
                       state_ref, ubuf_ref, hn_ref, *piece_refs, tm, staged):
    C = HGRN_CHUNK
    P = EVEN_PIECE
    levels = _hgrn_levels(C)
    proj_refs = piece_refs[:tm // P]
    ymix_refs = piece_refs[tm // P:]

    @pl.when((pl.program_id(0) == 0) & (pl.program_id(1) == 0))
    def _():
        staged[0].load(win_hbm, win_ref, win_stage, win_sem)
        staged[1].load(wout_hbm, wout_ref, wout_stage, wout_sem)

    @pl.when(pl.program_id(1) == 0)
    def _():
        state_ref[...] = jnp.zeros_like(state_ref)
        ubuf_ref[0:SUBLANES, :] = jnp.zeros((SUBLANES, CONV_DIM), F32)

    bg = _Background()

    def project_steps(k):
        hn_ref[...] = _rmsnorm(h_ref[0, k * P:(k + 1) * P, :], nw_ref[...]).astype(BF16)

        def step(c0):
            def run():
                proj_refs[k][:, c0:c0 + PROJ_GROUP] = jnp.dot(hn_ref[...], win_ref[:, c0:c0 + PROJ_GROUP],
                                                              preferred_element_type=F32)
            return run
        return [step(c0) for c0 in range(0, EVEN_IN, PROJ_GROUP)]

    def out_steps(k):
        def step(c0):
            def run():
                o_ref[0, k * P:(k + 1) * P, c0:c0 + PROJ_GROUP] = (
                    h_ref[0, k * P:(k + 1) * P, c0:c0 + PROJ_GROUP]
                    + jnp.dot(ymix_refs[k][...], wout_ref[:, c0:c0 + PROJ_GROUP],
                              preferred_element_type=F32))
            return run
        return [step(c0) for c0 in range(0, D_MODEL, PROJ_GROUP)]

    ri = lax.broadcasted_iota(jnp.int32, (C, C), 0)
    cj = lax.broadcasted_iota(jnp.int32, (C, C), 1)
    xor = ri ^ cj
    lower = ri > cj
    level_masks = [lower & (xor >= s) & (xor < 2 * s) for s in levels]
    eye = ri == cj

    lb = lb_ref[...]
    one_m_lb = 1.0 - lb
    log_one_m_lb = jnp.log(one_m_lb)
    tri = (ri >= cj).astype(BF16)
    cw = convw_ref[...]
    qo = 3 * CONV_DIM

    def hgrn_chunk(k, rows):
        proj_ref = proj_refs[k]
        qx = proj_ref[rows, qo:qo + HGRN_DIM]
        fx = proj_ref[rows, qo + HGRN_DIM:qo + 2 * HGRN_DIM]
        iv = proj_ref[rows, qo + 2 * HGRN_DIM:qo + 3 * HGRN_DIM]
        gx = proj_ref[rows, qo + 3 * HGRN_DIM:qo + 4 * HGRN_DIM]
        qf = _silu(qx)
        sig, nsig = _sigmoid_pair(fx)
        f = lb + one_m_lb * sig
        logf2 = jnp.where(f > 0, jnp.log(f), log_one_m_lb + fx) * LOG2_E
        kf = one_m_lb * nsig
        hi, mid = _split2_bf16(logf2)
        bcum = (jnp.dot(tri, hi, preferred_element_type=F32)
                + jnp.dot(tri, mid, preferred_element_type=F32))
        b3 = bcum.reshape(C // SUBLANES, SUBLANES, HGRN_DIM)
        blast = bcum[C - 1:C, :]
        qb = qf.astype(BF16)
        kb = kf.astype(BF16)
        vb = iv.astype(BF16)
        qdec = qb * jnp.exp2(bcum).astype(BF16)
        kdec = kb * jnp.exp2(blast - bcum).astype(BF16)
        sdec = jnp.exp2(blast)
        qk = qf * kf
        gate = _silu(gx)
        qs, ks = [], []
        for expo in _hgrn_level_exponents(b3):
            w = jnp.exp2(expo).reshape(C, HGRN_DIM).astype(BF16)
            qs.append(qb * w)
            ks.append(kb * w)
        heads = [slice(hd * HGRN_DK, (hd + 1) * HGRN_DK) for hd in range(HGRN_HEADS)]
        scores = []
        for sl in heads:
            bg.drain(next(bg_pattern))
            sc = jnp.zeros((C, C), F32)
            for li in range(len(levels)):
                p = lax.dot_general(qs[li][:, sl], ks[li][:, sl], _NT, preferred_element_type=F32)
                sc = jnp.where(level_masks[li], p, sc)
            diag = jnp.sum(qk[:, sl], axis=-1, keepdims=True)
            scores.append(jnp.where(eye, diag, sc).astype(BF16))
        for hd, sl in enumerate(heads):
            st = state_ref[hd]
            o = (jnp.dot(scores[hd], vb[:, sl], preferred_element_type=F32)
                 + lax.dot_general(qdec[:, sl], st.astype(BF16), _NT, preferred_element_type=F32))
            state_ref[hd] = sdec[:, sl] * st + lax.dot_general(vb[:, sl], kdec[:, sl], _TN,
                                                                preferred_element_type=F32)
            y = _rmsnorm(o, hnw_ref[:, sl]) * gate[:, sl]
            ymix_refs[k][rows, CONV_DIM + hd * HGRN_DV:CONV_DIM + (hd + 1) * HGRN_DV] = y.astype(BF16)

    bg.add(project_steps(0))
    bg.drain()
    for k in range(tm // P):
        if k + 1 < tm // P:
            bg.add(project_steps(k + 1))
        bg_pattern = iter(EVEN_BG_PATTERN)
        u0 = SUBLANES + k * P
        proj_ref = proj_refs[k]
        u = proj_ref[:, 2 * CONV_DIM:3 * CONV_DIM] * proj_ref[:, 0:CONV_DIM]
        ubuf_ref[u0:u0 + P, :] = u
        u1 = ubuf_ref[pl.ds(u0 - 1, P), :]
        u2 = ubuf_ref[pl.ds(u0 - 2, P), :]
        yconv = proj_ref[:, CONV_DIM:2 * CONV_DIM] * (u2 * cw[0:1, :] + u1 * cw[1:2, :] + u * cw[2:3, :])
        ymix_refs[k][:, 0:CONV_DIM] = yconv.astype(BF16)
        for ci in range(P // C):
            hgrn_chunk(k, slice(ci * C, (ci + 1) * C))
        bg.drain()
        bg.add(out_steps(k))
    bg.drain()
    ubuf_ref[0:SUBLANES, :] = ubuf_ref[tm:tm + SUBLANES, :]


def _even_mixer(h, norm_w, w_in, conv_w, lb, hgrn_norm_w, w_out):
    B, S, D = h.shape
    tm = TM_EVEN
    n_pieces = tm // EVEN_PIECE
    staged = (_StagedWeight(D, EVEN_IN, 64), _StagedWeight(EVEN_MIX, D, 128))
    resident = (tm * EVEN_IN * 4 + 4 * tm * D * 4 + sum(s.vmem_bytes() for s in staged))
    return pl.pallas_call(
        functools.partial(_even_mixer_kernel, tm=tm, staged=staged),
        grid=(B, S // tm),
        in_specs=[
            pl.BlockSpec((1, tm, D), lambda b, t: (b, t, 0)),
            _resident((1, D)),
            pl.BlockSpec(memory_space=pl.ANY),
            _resident((CONV_WIDTH, CONV_DIM)),
            _resident((1, HGRN_DIM)),
            _resident((1, HGRN_DIM)),
            pl.BlockSpec(memory_space=pl.ANY),
        ],
        out_specs=pl.BlockSpec((1, tm, D), lambda b, t: (b, t, 0)),
        out_shape=jax.ShapeDtypeStruct((B, S, D), F32),
        scratch_shapes=staged[0].scratch() + staged[1].scratch() + [
            pltpu.VMEM((HGRN_HEADS, HGRN_DV, HGRN_DK), F32),
            pltpu.VMEM((tm + SUBLANES, CONV_DIM), F32),
            pltpu.VMEM((EVEN_PIECE, D), BF16),
        ] + [pltpu.VMEM((EVEN_PIECE, EVEN_IN), F32)] * n_pieces
          + [pltpu.VMEM((EVEN_PIECE, EVEN_MIX), BF16)] * n_pieces,
        compiler_params=pltpu.CompilerParams(
            dimension_semantics=("arbitrary", "arbitrary"),
            vmem_limit_bytes=_vmem_limit(resident)),
        name="even_mixer",
    )(h, norm_w, w_in, conv_w, lb, hgrn_norm_w, w_out)


def _odd_mixer_kernel(cdec_ref, h_ref, pos_ref, nw_ref, win_hbm, invf_ref, dmat_ref, qdec_ref, kdec_ref,
                      gnw_ref, wout_hbm, o_ref, win_ref, win_stage, win_sem, wout_ref, wout_stage, wout_sem,
                      state_ref, hn_ref, *piece_refs, tm, staged):
    P = ODD_PIECE
    assert P == RET_CHUNK
    half = RET_DK // 2
    proj_refs = piece_refs[:PIECE_BUFFERS]
    ymix_refs = piece_refs[PIECE_BUFFERS:]
    bg = _Background()
    PROJ_GROUP = ODD_PROJ_GROUP

    @pl.when((pl.program_id(0) == 0) & (pl.program_id(1) == 0))
    def _():
        staged[0].load(win_hbm, win_ref, win_stage, win_sem)
        staged[1].load(wout_hbm, wout_ref, wout_stage, wout_sem)

    @pl.when(pl.program_id(1) == 0)
    def _():
        state_ref[...] = jnp.zeros_like(state_ref)

    def project_steps(k):
        hn_ref[...] = _rmsnorm(h_ref[0, k * P:(k + 1) * P, :], nw_ref[...]).astype(BF16)

        def step(c0):
            def run():
                proj_refs[k % PIECE_BUFFERS][:, c0:c0 + PROJ_GROUP] = jnp.dot(
                    hn_ref[...], win_ref[:, c0:c0 + PROJ_GROUP], preferred_element_type=F32)
            return run
        return [step(c0) for c0 in range(0, ODD_IN, PROJ_GROUP)]

    def out_steps(k):
        def step(c0):
            def run():
                o_ref[0, k * P:(k + 1) * P, c0:c0 + OUT_GROUP] = (
                    h_ref[0, k * P:(k + 1) * P, c0:c0 + OUT_GROUP]
                    + jnp.dot(ymix_refs[k % PIECE_BUFFERS][...], wout_ref[:, c0:c0 + OUT_GROUP],
                              preferred_element_type=F32))
            return run
        return [step(c0) for c0 in range(0, D_MODEL, OUT_GROUP)]

    def retention_chunk(k):
        proj_ref = proj_refs[k % PIECE_BUFFERS]
        ang = pos_ref[0, k * P:(k + 1) * P, :] * invf_ref[...]
        cs = jnp.cos(ang)
        sn = jnp.sin(ang)

        def rotate(col0):
            x1 = proj_ref[:, col0:col0 + half]
            x2 = proj_ref[:, col0 + half:col0 + 2 * half]
            return jnp.concatenate([x1 * cs - x2 * sn, x1 * sn + x2 * cs], axis=-1).astype(BF16)

        qd, kd, scores = [], [], []
        for hd in range(RET_HEADS):
            bg.drain(next(bg_pattern))
            q = rotate(hd * RET_DK)
            kk = rotate(RET_K_DIM + hd * RET_DK)
            sc = lax.dot_general(q, kk, _NT, preferred_element_type=F32) * dmat_ref[hd]
            scores.append(sc.astype(BF16))
            qd.append(q * qdec_ref[hd])
            kd.append(kk * kdec_ref[hd])
        for hd in range(RET_HEADS):
            bg.drain(next(bg_pattern))
            vo = 2 * RET_K_DIM + hd * RET_DV
            go = 2 * RET_K_DIM + RET_V_DIM + hd * RET_DV
            v = proj_ref[:, vo:vo + RET_DV].astype(BF16)
            g = proj_ref[:, go:go + RET_DV]
            st = state_ref[hd]
            o = (jnp.dot(scores[hd], v, preferred_element_type=F32)
                 + jnp.dot(qd[hd], st.astype(BF16), preferred_element_type=F32))
            state_ref[hd] = cdec_ref[hd] * st + lax.dot_general(kd[hd], v, _TN, preferred_element_type=F32)
            oc = o - jnp.mean(o, axis=-1, keepdims=True)
            y = oc * lax.rsqrt(jnp.mean(oc * oc, axis=-1, keepdims=True) + EPS)
            y = y * gnw_ref[:, hd * RET_DV:(hd + 1) * RET_DV] * _silu(g)
            ymix_refs[k % PIECE_BUFFERS][:, hd * RET_DV:(hd + 1) * RET_DV] = y.astype(BF16)

    bg.add(project_steps(0))
    bg.drain()
    for k in range(tm // P):
        if k + 1 < tm // P:
            bg.add(project_steps(k + 1))
        bg_pattern = iter(ODD_BG_PATTERN)
        retention_chunk(k)
        bg.drain()
        bg.add(out_steps(k))
    bg.drain()


def _retention_tables():
    C = RET_CHUNK
    log_gamma = jnp.log1p(-jnp.exp2(-5.0 - jnp.arange(RET_HEADS, dtype=F32)))
    idx = jnp.arange(C, dtype=F32)
    rel = idx[:, None] - idx[None, :]
    kscale = RET_DK ** -0.5
    dmat = jnp.where(rel >= 0, jnp.exp(log_gamma[:, None, None] * jnp.maximum(rel, 0.0)), 0.0) * kscale
    qdec = jnp.exp(log_gamma[:, None] * (idx[None, :] + 1.0))
    kdec = jnp.exp(log_gamma[:, None] * (C - 1.0 - idx[None, :])) * kscale
    cdec = jnp.exp(log_gamma * C)
    widen = lambda t: jnp.broadcast_to(t[:, :, None], (RET_HEADS, C, RET_DK)).astype(BF16)
    return dmat, widen(qdec), widen(kdec), cdec


def _odd_mixer(h, pos, norm_w, w_in, gn_w, w_out):
    B, S, D = h.shape
    tm = TM_ODD
    half = RET_DK // 2
    inv_freq = (ROPE_BASE ** (-jnp.arange(half, dtype=F32) / half)).reshape(1, half)
    dmat, qdec, kdec, cdec = _retention_tables()
    staged = (_StagedWeight(D, ODD_IN, 32), _StagedWeight(RET_V_DIM, D, 128))
    resident = (PIECE_BUFFERS * ODD_PIECE * (ODD_IN * 4 + RET_V_DIM * 2) + 4 * tm * D * 4
                + RET_HEADS * RET_DK * RET_DV * 4
                + sum(s.vmem_bytes() for s in staged))
    return pl.pallas_call(
        functools.partial(_odd_mixer_kernel, tm=tm, staged=staged),
        grid=(B, S // tm),
        in_specs=[
            pl.BlockSpec(memory_space=pltpu.SMEM),
            pl.BlockSpec((1, tm, D), lambda b, t: (b, t, 0)),
            pl.BlockSpec((1, tm, 1), lambda b, t: (b, t, 0)),
            _resident((1, D)),
            pl.BlockSpec(memory_space=pl.ANY),
            _resident((1, half)),
            _resident(dmat.shape),
            _resident(qdec.shape),
            _resident(kdec.shape),
            _resident((1, RET_V_DIM)),
            pl.BlockSpec(memory_space=pl.ANY),
        ],
        out_specs=pl.BlockSpec((1, tm, D), lambda b, t: (b, t, 0)),
        out_shape=jax.ShapeDtypeStruct((B, S, D), F32),
        scratch_shapes=staged[0].scratch() + staged[1].scratch() + [
            pltpu.VMEM((RET_HEADS, RET_DK, RET_DV), F32),
            pltpu.VMEM((ODD_PIECE, D), BF16),
        ] + [pltpu.VMEM((ODD_PIECE, ODD_IN), F32)] * PIECE_BUFFERS
          + [pltpu.VMEM((ODD_PIECE, RET_V_DIM), BF16)] * PIECE_BUFFERS,
        compiler_params=pltpu.CompilerParams(
            dimension_semantics=("arbitrary", "arbitrary"),
            vmem_limit_bytes=_vmem_limit(resident)),
        name="odd_mixer",
    )(cdec, h, pos, norm_w, w_in, inv_freq, dmat, qdec, kdec, gn_w, w_out)


def _ffn_kernel(h_ref, nw_ref, win_hbm, wout_hbm, fnw_ref, o_ref, act_ref,
                win_ref, win_stage, win_sem, wout_ref, wout_stage, wout_sem, *, final_norm, layer, staged):
    @pl.when(pl.program_id(0) == 0)
    def _():
        staged[0].load(win_hbm.at[layer], win_ref, win_stage, win_sem)
        staged[1].load(wout_hbm.at[layer], wout_ref, wout_stage, wout_sem)

    x = h_ref[...]
    hn = _rmsnorm(x, nw_ref[...]).astype(BF16)
    for c0 in range(0, FFN_HIDDEN, FFN_GROUP):
        gate = lax.dot_general(hn, win_ref[:, c0:c0 + FFN_GROUP], _NN, preferred_element_type=F32)
        up = lax.dot_general(hn, win_ref[:, FFN_HIDDEN + c0:FFN_HIDDEN + c0 + FFN_GROUP], _NN,
                             preferred_element_type=F32)
        act_ref[:, c0:c0 + FFN_GROUP] = (_silu(gate) * up).astype(BF16)
    y = x + lax.dot_general(act_ref[...], wout_ref[...], _NN, preferred_element_type=F32)
    if final_norm:
        y = _rmsnorm(y, fnw_ref[...])
    o_ref[...] = y


def _ffn(h2d, norm_w, w_in, w_out, final_w, layer, final_norm):
    T, D = h2d.shape
    tm = TM_FFN
    assert FFN_HIDDEN % FFN_GROUP == 0
    staged = (_StagedWeight(D, 2 * FFN_HIDDEN, 64), _StagedWeight(FFN_HIDDEN, D, 128))
    resident = 4 * tm * D * 4 + tm * FFN_HIDDEN * 2 + sum(s.vmem_bytes() for s in staged)
    return pl.pallas_call(
        functools.partial(_ffn_kernel, final_norm=final_norm, layer=layer, staged=staged),
        grid=(T // tm,),
        in_specs=[
            pl.BlockSpec((tm, D), lambda i: (i, 0)),
            _resident((1, D)),
            pl.BlockSpec(memory_space=pl.ANY),
            pl.BlockSpec(memory_space=pl.ANY),
            _resident((1, D)),
        ],
        out_specs=pl.BlockSpec((tm, D), lambda i: (i, 0)),
        out_shape=jax.ShapeDtypeStruct((T, D), F32),
        scratch_shapes=[pltpu.VMEM((tm, FFN_HIDDEN), BF16)]
                       + staged[0].scratch() + staged[1].scratch(),
        compiler_params=pltpu.CompilerParams(
            dimension_semantics=("arbitrary",),
            vmem_limit_bytes=_vmem_limit(resident)),
        name="ffn_final" if final_norm else "ffn",
    )(h2d, norm_w, w_in, w_out, final_w)


def kernel(x, positions, norm_mix_w, norm_ffn_w, norm_final_w, even_w_in, conv_w, hgrn_lb_logits, hgrn_norm_w,
           even_w_out, odd_w_in, ret_norm_w, odd_w_out, ffn_w_in, ffn_w_out):
    B, S, D = x.shape
    assert D == D_MODEL and S % TM_EVEN == 0 and S % TM_ODD == 0 and (B * S) % TM_FFN == 0
    assert norm_mix_w.shape[0] == 2 and even_w_in.shape[0] == 1 and odd_w_in.shape[0] == 1
    lbs = jnp.cumsum(jax.nn.softmax(hgrn_lb_logits.astype(F32), axis=0), axis=0)
    pos = positions.astype(F32).reshape(B, S, 1)
    row = lambda w: w.reshape(1, -1).astype(F32)

    h = _even_mixer(x, row(norm_mix_w[0]), even_w_in[0], conv_w[0], row(lbs[0]),
                    row(hgrn_norm_w[0]), even_w_out[0])
    h = _ffn(h.reshape(B * S, D), row(norm_ffn_w[0]), ffn_w_in, ffn_w_out,
             row(norm_final_w), 0, False).reshape(B, S, D)
    h = _odd_mixer(h, pos, row(norm_mix_w[1]), odd_w_in[0], row(ret_norm_w[0]), odd_w_out[0])
    h = _ffn(h.reshape(B * S, D), row(norm_ffn_w[1]), ffn_w_in, ffn_w_out,
             row(norm_final_w), 1, True)
    return h.reshape(B, S, D)
```

```python
import functools

import jax
import jax.numpy as jnp
from jax import lax
from jax.experimental import pallas as pl
from jax.experimental.pallas import tpu as pltpu

F32 = jnp.float32
BF16 = jnp.bfloat16

D_MODEL = 1024
EPS = 1e-6
LOG2_E = 1.4426950408889634

CONV_DIM = 512
CONV_WIDTH = 3
HGRN_HEADS = 4
HGRN_DK = 128
HGRN_DV = 128
HGRN_DIM = HGRN_HEADS * HGRN_DK
EVEN_IN = 3 * CONV_DIM + 4 * HGRN_DIM
EVEN_MIX = CONV_DIM + HGRN_DIM
HGRN_CHUNK = 64

RET_HEADS = 4
RET_DK = 256
RET_DV = 512
RET_K_DIM = RET_HEADS * RET_DK
RET_V_DIM = RET_HEADS * RET_DV
ODD_IN = 2 * RET_K_DIM + 2 * RET_V_DIM
RET_CHUNK = 128
ROPE_BASE = 10000.0

FFN_HIDDEN = 2816

V7X_VMEM_BYTES = 64 * 1024 * 1024
SUBLANES = 8
VMEM_RESERVE_BYTES = 4 * 1024 * 1024
TEMPORARIES_BYTES = 24 * 1024 * 1024

TM_EVEN = 1024
TM_ODD = 512
TM_FFN = 1024
FFN_GROUP = 256
ODD_PIECE = 128
EVEN_PIECE = 128
PIECE_BUFFERS = 2
PROJ_GROUP = 256
ODD_PROJ_GROUP = 256
OUT_GROUP = 256
EVEN_BG_PATTERN = (3, 3, 3, 3, 3, 3, 0, 0)
ODD_BG_PATTERN = (2, 2, 2, 2, 0, 0, 0, 0)
STAGE_SLOTS = 8

_NN = (((1,), (0,)), ((), ()))
_NT = (((1,), (1,)), ((), ()))
_TN = (((0,), (0,)), ((), ()))


def _vmem_limit(resident_bytes):
    return int(min(V7X_VMEM_BYTES - VMEM_RESERVE_BYTES, resident_bytes + TEMPORARIES_BYTES))


def _resident(shape):
    zeros = (0,) * len(shape)
    return pl.BlockSpec(shape, lambda *_: zeros, pipeline_mode=pl.Buffered(1))


class _Background:
    def __init__(self):
        self._steps = []

    def add(self, steps):
        self._steps.extend(steps)

    def drain(self, n=None):
        for _ in range(len(self._steps) if n is None else min(n, len(self._steps))):
            self._steps.pop(0)()


class _StagedWeight:
    def __init__(self, rows, cols, chunk):
        assert rows % chunk == 0
        self.rows, self.cols, self.chunk = rows, cols, chunk

    def vmem_bytes(self):
        return self.rows * self.cols * 2 + STAGE_SLOTS * self.chunk * self.cols * 4

    def scratch(self):
        return [pltpu.VMEM((self.rows, self.cols), BF16),
                pltpu.VMEM((STAGE_SLOTS, self.chunk, self.cols), F32),
                pltpu.SemaphoreType.DMA((STAGE_SLOTS,))]

    def load(self, src_hbm, dst_ref, stage_ref, sem_ref):
        n = self.rows // self.chunk
        ahead = STAGE_SLOTS - 1

        def copy(i):
            return pltpu.make_async_copy(src_hbm.at[pl.ds(i * self.chunk, self.chunk), :],
                                         stage_ref.at[i % STAGE_SLOTS], sem_ref.at[i % STAGE_SLOTS])
        for i in range(min(ahead, n)):
            copy(i).start(priority=i % 2)
        for i in range(n):
            if i + ahead < n:
                copy(i + ahead).start(priority=(i + ahead) % 2)
            copy(i).wait()
            dst_ref[i * self.chunk:(i + 1) * self.chunk, :] = stage_ref[i % STAGE_SLOTS].astype(BF16)


def _rmsnorm(x, w):
    return x * lax.rsqrt(jnp.mean(x * x, axis=-1, keepdims=True) + EPS) * w


def _sigmoid_pair(x):
    e = jnp.exp(-jnp.abs(x))
    r = 1.0 / (1.0 + e)
    er = e * r
    pos = x >= 0
    return jnp.where(pos, r, er), jnp.where(pos, er, r)


def _silu(x):
    hx = 0.5 * x
    return hx + hx * jnp.tanh(hx)


def _hgrn_levels(chunk):
    s, out = chunk // 2, []
    while s >= 1:
        out.append(s)
        s //= 2
    return tuple(out)


def _hgrn_level_exponents(b3):
    G = b3.shape[0]
    tail = b3.shape[1:]
    last = jnp.broadcast_to(b3[:, SUBLANES - 1:SUBLANES, :], b3.shape)
    sub = lax.broadcasted_iota(jnp.int32, b3.shape, 1)
    out = []
    for s in _hgrn_levels(G * SUBLANES):
        if s >= SUBLANES:
            per = 2 * s // SUBLANES
            blocks = b3.reshape((G // per, per) + tail)
            ref = last.reshape((G // per, per) + tail)[:, per // 2 - 1:per // 2]
            below = ref - blocks[:, :per // 2]
            above = blocks[:, per // 2:] - ref
            out.append(jnp.concatenate([below, above], axis=1).reshape(b3.shape))
        elif s == 4:
            out.append(-jnp.abs(b3 - b3[:, 3:4, :]))
        elif s == 2:
            pos = sub & 3
            nxt1 = pltpu.roll(b3, SUBLANES - 1, 1)
            prev1 = pltpu.roll(b3, 1, 1)
            prev2 = pltpu.roll(b3, 2, 1)
            ref = jnp.where(pos == 0, nxt1, jnp.where(pos == 1, b3, jnp.where(pos == 2, prev1, prev2)))
            out.append(-jnp.abs(b3 - ref))
        else:
            out.append(-jnp.abs(b3 - jnp.where((sub & 1) == 1, pltpu.roll(b3, 1, 1), b3)))
    return out


def _split2_bf16(x):
    hi = x.astype(BF16)
    mid = (x - hi.astype(F32)).astype(BF16)
    return hi, mid


def _even_mixer_kernel(h_ref, nw_ref, win_hbm, convw_ref, lb_ref, hnw_ref, wout_hbm, o_ref,
                       win_ref, win_stage, win_sem, wout_ref, wout_stage, wout_sem,
                       state_ref, ubuf_ref, hn_ref, *piece_refs, tm, staged):
    C = HGRN_CHUNK
    P = EVEN_PIECE
    levels = _hgrn_levels(C)
    proj_refs = piece_refs[:tm // P]
    ymix_refs = piece_refs[tm // P:]

    @pl.when((pl.program_id(0) == 0) & (pl.program_id(1) == 0))
    def _():
        staged[0].load(win_hbm, win_ref, win_stage, win_sem)
        staged[1].load(wout_hbm, wout_ref, wout_stage, wout_sem)

    @pl.when(pl.program_id(1) == 0)
    def _():
        state_ref[...] = jnp.zeros_like(state_ref)
        ubuf_ref[0:SUBLANES, :] = jnp.zeros((SUBLANES, CONV_DIM), F32)

    bg = _Background()

    def project_steps(k):
        hn_ref[...] = _rmsnorm(h_ref[0, k * P:(k + 1) * P, :], nw_ref[...]).astype(BF16)

        def step(c0):
            def run():
                proj_refs[k][:, c0:c0 + PROJ_GROUP] = jnp.dot(hn_ref[...], win_ref[:, c0:c0 + PROJ_GROUP],
                                                              preferred_element_type=F32)
            return run
        return [step(c0) for c0 in range(0, EVEN_IN, PROJ_GROUP)]

    def out_steps(k):
        def step(c0):
            def run():
                o_ref[0, k * P:(k + 1) * P, c0:c0 + PROJ_GROUP] = (
                    h_ref[0, k * P:(k + 1) * P, c0:c0 + PROJ_GROUP]
                    + jnp.dot(ymix_refs[k][...], wout_ref[:, c0:c0 + PROJ_GROUP],
                              preferred_element_type=F32))
            return run
        return [step(c0) for c0 in range(0, D_MODEL, PROJ_GROUP)]

    ri = lax.broadcasted_iota(jnp.int32, (C, C), 0)
    cj = lax.broadcasted_iota(jnp.int32, (C, C), 1)
    xor = ri ^ cj
    lower = ri > cj
    level_masks = [lower & (xor >= s) & (xor < 2 * s) for s in levels]
    eye = ri == cj

    lb = lb_ref[...]
    one_m_lb = 1.0 - lb
    log_one_m_lb = jnp.log(one_m_lb)
    tri = (ri >= cj).astype(BF16)
    cw = convw_ref[...]
    qo = 3 * CONV_DIM

    def hgrn_chunk(k, rows):
        proj_ref = proj_refs[k]
        qx = proj_ref[rows, qo:qo + HGRN_DIM]
        fx = proj_ref[rows, qo + HGRN_DIM:qo + 2 * HGRN_DIM]
        iv = proj_ref[rows, qo + 2 * HGRN_DIM:qo + 3 * HGRN_DIM]
        gx = proj_ref[rows, qo + 3 * HGRN_DIM:qo + 4 * HGRN_DIM]
        qf = _silu(qx)
        sig, nsig = _sigmoid_pair(fx)
        f = lb + one_m_lb * sig
        logf2 = jnp.where(f > 0, jnp.log(f), log_one_m_lb + fx) * LOG2_E
        kf = one_m_lb * nsig
        hi, mid = _split2_bf16(logf2)
        bcum = (jnp.dot(tri, hi, preferred_element_type=F32)
                + jnp.dot(tri, mid, preferred_element_type=F32))
        b3 = bcum.reshape(C // SUBLANES, SUBLANES, HGRN_DIM)
        blast = bcum[C - 1:C, :]
        qb = qf.astype(BF16)
        kb = kf.astype(BF16)
        vb = iv.astype(BF16)
        qdec = qb * jnp.exp2(bcum).astype(BF16)
        kdec = kb * jnp.exp2(blast - bcum).astype(BF16)
        sdec = jnp.exp2(blast)
        qk = qf * kf
        gate = _silu(gx)
        qs, ks = [], []
        for expo in _hgrn_level_exponents(b3):
            w = jnp.exp2(expo).reshape(C, HGRN_DIM).astype(BF16)
            qs.append(qb * w)
            ks.append(kb * w)
        heads = [slice(hd * HGRN_DK, (hd + 1) * HGRN_DK) for hd in range(HGRN_HEADS)]
        scores = []
        for sl in heads:
            bg.drain(next(bg_pattern))
            sc = jnp.zeros((C, C), F32)
            for li in range(len(levels)):
                p = lax.dot_general(qs[li][:, sl], ks[li][:, sl], _NT, preferred_element_type=F32)
                sc = jnp.where(level_masks[li], p, sc)
            diag = jnp.sum(qk[:, sl], axis=-1, keepdims=True)
            scores.append(jnp.where(eye, diag, sc).astype(BF16))
        for hd, sl in enumerate(heads):
            st = state_ref[hd]
            o = (jnp.dot(scores[hd], vb[:, sl], preferred_element_type=F32)
                 + lax.dot_general(qdec[:, sl], st.astype(BF16), _NT, preferred_element_type=F32))
            state_ref[hd] = sdec[:, sl] * st + lax.dot_general(vb[:, sl], kdec[:, sl], _TN,
                                                                preferred_element_type=F32)
            y = _rmsnorm(o, hnw_ref[:, sl]) * gate[:, sl]
            ymix_refs[k][rows, CONV_DIM + hd * HGRN_DV:CONV_DIM + (hd + 1) * HGRN_DV] = y.astype(BF16)

    bg.add(project_steps(0))
    bg.drain()
    for k in range(tm // P):
        if k + 1 < tm // P:
            bg.add(project_steps(k + 1))
        bg_pattern = iter(EVEN_BG_PATTERN)
        u0 = SUBLANES + k * P
        proj_ref = proj_refs[k]
        u = proj_ref[:, 2 * CONV_DIM:3 * CONV_DIM] * proj_ref[:, 0:CONV_DIM]
        ubuf_ref[u0:u0 + P, :] = u
        u1 = ubuf_ref[pl.ds(u0 - 1, P), :]
        u2 = ubuf_ref[pl.ds(u0 - 2, P), :]
        yconv = proj_ref[:, CONV_DIM:2 * CONV_DIM] * (u2 * cw[0:1, :] + u1 * cw[1:2, :] + u * cw[2:3, :])
        ymix_refs[k][:, 0:CONV_DIM] = yconv.astype(BF16)
        for ci in range(P // C):
            hgrn_chunk(k, slice(ci * C, (ci + 1) * C))
        bg.drain()
        bg.add(out_steps(k))
    bg.drain()
    ubuf_ref[0:SUBLANES, :] = ubuf_ref[tm:tm + SUBLANES, :]


def _even_mixer(h, norm_w, w_in, conv_w, lb, hgrn_norm_w, w_out):
    B, S, D = h.shape
    tm = TM_EVEN
    n_pieces = tm // EVEN_PIECE
    staged = (_StagedWeight(D, EVEN_IN, 64), _StagedWeight(EVEN_MIX, D, 128))
    resident = (tm * EVEN_IN * 4 + 4 * tm * D * 4 + sum(s.vmem_bytes() for s in staged))
    return pl.pallas_call(
        functools.partial(_even_mixer_kernel, tm=tm, staged=staged),
        grid=(B, S // tm),
        in_specs=[
            pl.BlockSpec((1, tm, D), lambda b, t: (b, t, 0)),
            _resident((1, D)),
            pl.BlockSpec(memory_space=pl.ANY),
            _resident((CONV_WIDTH, CONV_DIM)),
            _resident((1, HGRN_DIM)),
            _resident((1, HGRN_DIM)),
            pl.BlockSpec(memory_space=pl.ANY),
        ],
        out_specs=pl.BlockSpec((1, tm, D), lambda b, t: (b, t, 0)),
        out_shape=jax.ShapeDtypeStruct((B, S, D), F32),
        scratch_shapes=staged[0].scratch() + staged[1].scratch() + [
            pltpu.VMEM((HGRN_HEADS, HGRN_DV, HGRN_DK), F32),
            pltpu.VMEM((tm + SUBLANES, CONV_DIM), F32),
            pltpu.VMEM((EVEN_PIECE, D), BF16),
        ] + [pltpu.VMEM((EVEN_PIECE, EVEN_IN), F32)] * n_pieces
          + [pltpu.VMEM((EVEN_PIECE, EVEN_MIX), BF16)] * n_pieces,
        compiler_params=pltpu.CompilerParams(
            dimension_semantics=("arbitrary", "arbitrary"),
            vmem_limit_bytes=_vmem_limit(resident)),
        name="even_mixer",
    )(h, norm_w, w_in, conv_w, lb, hgrn_norm_w, w_out)


def _odd_mixer_kernel(cdec_ref, h_ref, pos_ref, nw_ref, win_hbm, invf_ref, dmat_ref, qdec_ref, kdec_ref,
                      gnw_ref, wout_hbm, o_ref, win_ref, win_stage, win_sem, wout_ref, wout_stage, wout_sem,
                      state_ref, hn_ref, *piece_refs, tm, staged):
    P = ODD_PIECE
    assert P == RET_CHUNK
    half = RET_DK // 2
    proj_refs = piece_refs[:PIECE_BUFFERS]
    ymix_refs = piece_refs[PIECE_BUFFERS:]
    bg = _Background()
    PROJ_GROUP = ODD_PROJ_GROUP

    @pl.when((pl.program_id(0) == 0) & (pl.program_id(1) == 0))
    def _():
        staged[0].load(win_hbm, win_ref, win_stage, win_sem)
        staged[1].load(wout_hbm, wout_ref, wout_stage, wout_sem)

    @pl.when(pl.program_id(1) == 0)
    def _():
        state_ref[...] = jnp.zeros_like(state_ref)

    def project_steps(k):
        hn_ref[...] = _rmsnorm(h_ref[0, k * P:(k + 1) * P, :], nw_ref[...]).astype(BF16)

        def step(c0):
            def run():
                proj_refs[k % PIECE_BUFFERS][:, c0:c0 + PROJ_GROUP] = jnp.dot(
                    hn_ref[...], win_ref[:, c0:c0 + PROJ_GROUP], preferred_element_type=F32)
            return run
        return [step(c0) for c0 in range(0, ODD_IN, PROJ_GROUP)]

    def out_steps(k):
        def step(c0):
            def run():
                o_ref[0, k * P:(k + 1) * P, c0:c0 + OUT_GROUP] = (
                    h_ref[0, k * P:(k + 1) * P, c0:c0 + OUT_GROUP]
                    + jnp.dot(ymix_refs[k % PIECE_BUFFERS][...], wout_ref[:, c0:c0 + OUT_GROUP],
                              preferred_element_type=F32))
            return run
        return [step(c0) for c0 in range(0, D_MODEL, OUT_GROUP)]

    def retention_chunk(k):
        proj_ref = proj_refs[k % PIECE_BUFFERS]
        ang = pos_ref[0, k * P:(k + 1) * P, :] * invf_ref[...]
        cs = jnp.cos(ang)
        sn = jnp.sin(ang)

        def rotate(col0):
            x1 = proj_ref[:, col0:col0 + half]
            x2 = proj_ref[:, col0 + half:col0 + 2 * half]
            return jnp.concatenate([x1 * cs - x2 * sn, x1 * sn + x2 * cs], axis=-1).astype(BF16)

        qd, kd, scores = [], [], []
        for hd in range(RET_HEADS):
            bg.drain(next(bg_pattern))
            q = rotate(hd * RET_DK)
            kk = rotate(RET_K_DIM + hd * RET_DK)
            sc = lax.dot_general(q, kk, _NT, preferred_element_type=F32) * dmat_ref[hd]
            scores.append(sc.astype(BF16))
            qd.append(q * qdec_ref[hd])
            kd.append(kk * kdec_ref[hd])
        for hd in range(RET_HEADS):
            bg.drain(next(bg_pattern))
            vo = 2 * RET_K_DIM + hd * RET_DV
            go = 2 * RET_K_DIM + RET_V_DIM + hd * RET_DV
            v = proj_ref[:, vo:vo + RET_DV].astype(BF16)
            g = proj_ref[:, go:go + RET_DV]
            st = state_ref[hd]
            o = (jnp.dot(scores[hd], v, preferred_element_type=F32)
                 + jnp.dot(qd[hd], st.astype(BF16), preferred_element_type=F32))
            state_ref[hd] = cdec_ref[hd] * st + lax.dot_general(kd[hd], v, _TN, preferred_element_type=F32)
            oc = o - jnp.mean(o, axis=-1, keepdims=True)
            y = oc * lax.rsqrt(jnp.mean(oc * oc, axis=-1, keepdims=True) + EPS)
            y = y * gnw_ref[:, hd * RET_DV:(hd + 1) * RET_DV] * _silu(g)
            ymix_refs[k % PIECE_BUFFERS][:, hd * RET_DV:(hd + 1) * RET_DV] = y.astype(BF16)

    bg.add(project_steps(0))
    bg.drain()
    for k in range(tm // P):
        if k + 1 < tm // P:
            bg.add(project_steps(k + 1))
        bg_pattern = iter(ODD_BG_PATTERN)
        retention_chunk(k)
        bg.drain()
        bg.add(out_steps(k))
    bg.drain()


def _retention_tables():
    C = RET_CHUNK
    log_gamma = jnp.log1p(-jnp.exp2(-5.0 - jnp.arange(RET_HEADS, dtype=F32)))
    idx = jnp.arange(C, dtype=F32)
    rel = idx[:, None] - idx[None, :]
    kscale = RET_DK ** -0.5
    dmat = jnp.where(rel >= 0, jnp.exp(log_gamma[:, None, None] * jnp.maximum(rel, 0.0)), 0.0) * kscale
    qdec = jnp.exp(log_gamma[:, None] * (idx[None, :] + 1.0))
    kdec = jnp.exp(log_gamma[:, None] * (C - 1.0 - idx[None, :])) * kscale
    cdec = jnp.exp(log_gamma * C)
    widen = lambda t: jnp.broadcast_to(t[:, :, None], (RET_HEADS, C, RET_DK)).astype(BF16)
    return dmat, widen(qdec), widen(kdec), cdec


def _odd_mixer(h, pos, norm_w, w_in, gn_w, w_out):
    B, S, D = h.shape
    tm = TM_ODD
    half = RET_DK // 2
    inv_freq = (ROPE_BASE ** (-jnp.arange(half, dtype=F32) / half)).reshape(1, half)
    dmat, qdec, kdec, cdec = _retention_tables()
    staged = (_StagedWeight(D, ODD_IN, 32), _StagedWeight(RET_V_DIM, D, 128))
    resident = (PIECE_BUFFERS * ODD_PIECE * (ODD_IN * 4 + RET_V_DIM * 2) + 4 * tm * D * 4
                + RET_HEADS * RET_DK * RET_DV * 4
                + sum(s.vmem_bytes() for s in staged))
    return pl.pallas_call(
        functools.partial(_odd_mixer_kernel, tm=tm, staged=staged),
        grid=(B, S // tm),
        in_specs=[
            pl.BlockSpec(memory_space=pltpu.SMEM),
            pl.BlockSpec((1, tm, D), lambda b, t: (b, t, 0)),
            pl.BlockSpec((1, tm, 1), lambda b, t: (b, t, 0)),
            _resident((1, D)),
            pl.BlockSpec(memory_space=pl.ANY),
            _resident((1, half)),
            _resident(dmat.shape),
            _resident(qdec.shape),
            _resident(kdec.shape),
            _resident((1, RET_V_DIM)),
            pl.BlockSpec(memory_space=pl.ANY),
        ],
        out_specs=pl.BlockSpec((1, tm, D), lambda b, t: (b, t, 0)),
        out_shape=jax.ShapeDtypeStruct((B, S, D), F32),
        scratch_shapes=staged[0].scratch() + staged[1].scratch() + [
            pltpu.VMEM((RET_HEADS, RET_DK, RET_DV), F32),
            pltpu.VMEM((ODD_PIECE, D), BF16),
        ] + [pltpu.VMEM((ODD_PIECE, ODD_IN), F32)] * PIECE_BUFFERS
          + [pltpu.VMEM((ODD_PIECE, RET_V_DIM), BF16)] * PIECE_BUFFERS,
        compiler_params=pltpu.CompilerParams(
            dimension_semantics=("arbitrary", "arbitrary"),
            vmem_limit_bytes=_vmem_limit(resident)),
        name="odd_mixer",
    )(cdec, h, pos, norm_w, w_in, inv_freq, dmat, qdec, kdec, gn_w, w_out)


def _ffn_kernel(h_ref, nw_ref, win_hbm, wout_hbm, fnw_ref, o_ref, act_ref,
                win_ref, win_stage, win_sem, wout_ref, wout_stage, wout_sem, *, final_norm, layer, staged):
    @pl.when(pl.program_id(0) == 0)
    def _():
        staged[0].load(win_hbm.at[layer], win_ref, win_stage, win_sem)
        staged[1].load(wout_hbm.at[layer], wout_ref, wout_stage, wout_sem)

    x = h_ref[...]
    hn = _rmsnorm(x, nw_ref[...]).astype(BF16)
    for c0 in range(0, FFN_HIDDEN, FFN_GROUP):
        gate = lax.dot_general(hn, win_ref[:, c0:c0 + FFN_GROUP], _NN, preferred_element_type=F32)
        up = lax.dot_general(hn, win_ref[:, FFN_HIDDEN + c0:FFN_HIDDEN + c0 + FFN_GROUP], _NN,
                             preferred_element_type=F32)
        act_ref[:, c0:c0 + FFN_GROUP] = (_silu(gate) * up).astype(BF16)
    y = x + lax.dot_general(act_ref[...], wout_ref[...], _NN, preferred_element_type=F32)
    if final_norm:
        y = _rmsnorm(y, fnw_ref[...])
    o_ref[...] = y


def _ffn(h2d, norm_w, w_in, w_out, final_w, layer, final_norm):
    T, D = h2d.shape
    tm = TM_FFN
    assert FFN_HIDDEN % FFN_GROUP == 0
    staged = (_StagedWeight(D, 2 * FFN_HIDDEN, 64), _StagedWeight(FFN_HIDDEN, D, 128))
    resident = 4 * tm * D * 4 + tm * FFN_HIDDEN * 2 + sum(s.vmem_bytes() for s in staged)
    return pl.pallas_call(
        functools.partial(_ffn_kernel, final_norm=final_norm, layer=layer, staged=staged),
        grid=(T // tm,),
        in_specs=[
            pl.BlockSpec((tm, D), lambda i: (i, 0)),
            _resident((1, D)),
            pl.BlockSpec(memory_space=pl.ANY),
            pl.BlockSpec(memory_space=pl.ANY),
            _resident((1, D)),
        ],
        out_specs=pl.BlockSpec((tm, D), lambda i: (i, 0)),
        out_shape=jax.ShapeDtypeStruct((T, D), F32),
        scratch_shapes=[pltpu.VMEM((tm, FFN_HIDDEN), BF16)]
                       + staged[0].scratch() + staged[1].scratch(),
        compiler_params=pltpu.CompilerParams(
            dimension_semantics=("arbitrary",),
            vmem_limit_bytes=_vmem_limit(resident)),
        name="ffn_final" if final_norm else "ffn",
    )(h2d, norm_w, w_in, w_out, final_w)


def kernel(x, positions, norm_mix_w, norm_ffn_w, norm_final_w, even_w_in, conv_w, hgrn_lb_logits, hgrn_norm_w,
           even_w_out, odd_w_in, ret_norm_w, odd_w_out, ffn_w_in, ffn_w_out):
    B, S, D = x.shape
    assert D == D_MODEL and S % TM_EVEN == 0 and S % TM_ODD == 0 and (B * S) % TM_FFN == 0
    assert norm_mix_w.shape[0] == 2 and even_w_in.shape[0] == 1 and odd_w_in.shape[0] == 1
    lbs = jnp.cumsum(jax.nn.softmax(hgrn_lb_logits.astype(F32), axis=0), axis=0)
    pos = positions.astype(F32).reshape(B, S, 1)
    row = lambda w: w.reshape(1, -1).astype(F32)

    h = _even_mixer(x, row(norm_mix_w[0]), even_w_in[0], conv_w[0], row(lbs[0]),
                    row(hgrn_norm_w[0]), even_w_out[0])
    h = _ffn(h.reshape(B * S, D), row(norm_ffn_w[0]), ffn_w_in, ffn_w_out,
             row(norm_final_w), 0, False).reshape(B, S, D)
    h = _odd_mixer(h, pos, row(norm_mix_w[1]), odd_w_in[0], row(ret_norm_w[0]), odd_w_out[0])
    h = _ffn(h.reshape(B * S, D), row(norm_ffn_w[1]), ffn_w_in, ffn_w_out,
             row(norm_final_w), 1, True)
    return h.reshape(B, S, D)
```
